```python
import math
import jax, jax.numpy as jnp
from jax import lax
import numpy as np

D_MODEL = 2048
BATCH = 8
SEQ = 2048
DEPTH = 1
DEC_BATCH = 2
DEC_SEQ = 8192
PAST_LEN = 128

N_Q_A = 16
N_KV_A = 2
HD_A = 64
GROUP_A = N_Q_A // N_KV_A
WINDOW = 128
N_H_B = 16
NOPE_B = 128
ROPE_B = 64
V_B = 128
Q_LORA = 512
KV_LORA = 512
ROPE_THETA = 10000.0
Q_BLOCK = 128
N_EXPERTS = 32
TOP_K = 4
D_FF = 2048
SWIGLU_LIMIT = 7.0
SWIGLU_ALPHA = 1.702
MOE_BLOCK = 128
LN_EPS = 1e-5
RMS_EPS = 1e-6
QA_W = N_Q_A * HD_A
KA_W = N_KV_A * HD_A
IN_WIDTHS = (QA_W, KA_W, KA_W, Q_LORA, KV_LORA, ROPE_B, D_MODEL, D_MODEL)
IN_W = QA_W + 2 * KA_W + Q_LORA + KV_LORA + ROPE_B + 2 * D_MODEL
N_MOD = 6

kernel_name = "hybrid_swa_mla_moe_deepnorm_adaln_encoder"


def _layer_norm(x, g, b):
    xf = x.astype(jnp.float32)
    mu = xf.mean(-1, keepdims=True)
    var = jnp.square(xf - mu).mean(-1, keepdims=True)
    return ((xf - mu) * lax.rsqrt(var + LN_EPS)).astype(x.dtype) * g + b


def _rms_norm(x, g):
    xf = x.astype(jnp.float32)
    r = lax.rsqrt(jnp.square(xf).mean(-1, keepdims=True) + RMS_EPS)
    return (xf * r).astype(x.dtype) * g


def _rope(x, pos):
    half = x.shape[-1] // 2
    freqs = ROPE_THETA ** (-jnp.arange(half, dtype=jnp.float32) / half)
    ang = pos[:, None] * freqs[None, :]
    cos = jnp.cos(ang)[:, None, :].astype(x.dtype)
    sin = jnp.sin(ang)[:, None, :].astype(x.dtype)
    x1, x2 = x[..., :half], x[..., half:]
    return jnp.concatenate([x1 * cos - x2 * sin, x2 * cos + x1 * sin], axis=-1)


def _window_gqa(q, k, v, sinks):
    B, S = q.shape[0], q.shape[1]
    nb = S // WINDOW
    qb = q.reshape(B, nb, WINDOW, N_KV_A, GROUP_A, HD_A)
    pad = ((0, 0), (WINDOW, WINDOW), (0, 0), (0, 0))
    kp, vp = jnp.pad(k, pad), jnp.pad(v, pad)

    def band(t):
        return jnp.concatenate([t[:, j * WINDOW:j * WINDOW + S].reshape(B, nb, WINDOW, N_KV_A, HD_A) for j in range(3)], axis=2)

    kb, vb = band(kp), band(vp)
    s = jnp.einsum('bnqhgd,bnkhd->bnhgqk', qb, kb).astype(jnp.float32) * (HD_A ** -0.5)
    rel = jnp.arange(3 * WINDOW)[None, :] - WINDOW
    dist = jnp.abs(jnp.arange(WINDOW)[:, None] - rel)
    kabs = jnp.arange(nb)[:, None] * WINDOW + rel
    valid = (dist <= WINDOW)[None] & ((kabs >= 0) & (kabs < S))[:, None, :]
    slopes = jnp.exp2(-8.0 * jnp.arange(1, N_Q_A + 1, dtype=jnp.float32) / N_Q_A)
    alibi = (-slopes[:, None, None] * dist.astype(jnp.float32)[None]).reshape(N_KV_A, GROUP_A, WINDOW, 3 * WINDOW)
    s = jnp.where(valid[None, :, None, None], s + alibi[None, None], -jnp.inf)
    sk = sinks.astype(jnp.float32).reshape(N_KV_A, GROUP_A)[None, None, :, :, None]
    m = jnp.maximum(s.max(-1), sk)
    p = jnp.exp(s - m[..., None])
    p = p / (p.sum(-1) + jnp.exp(sk - m))[..., None]
    o = jnp.einsum('bnhgqk,bnkhd->bnqhgd', p.astype(v.dtype), vb)
    return o.reshape(B, S, QA_W)


def _dense_attn_blocks(q, k, v):
    B, S, H, dq = q.shape
    nb = S // Q_BLOCK
    scale = dq ** -0.5
    qs = q.reshape(B, nb, Q_BLOCK, H, dq).transpose(1, 0, 2, 3, 4)

    def one(qblk):
        s = jnp.einsum('bqhd,bkhd->bhqk', qblk, k).astype(jnp.float32) * scale
        p = jax.nn.softmax(s, axis=-1)
        return jnp.einsum('bhqk,bkhd->bqhd', p.astype(v.dtype), v)

    o = lax.map(one, qs)
    return o.transpose(1, 0, 2, 3, 4).reshape(B, S, H * v.shape[-1])


def _moe(h, w_router, b_router, w_gu, b_gu, w_down, b_down):
    B, S, D = h.shape
    T = B * S
    xt = h.reshape(T, D)
    logits = (xt @ w_router + b_router).astype(jnp.float32)
    top_v, top_i = lax.top_k(logits, TOP_K)
    gates = jax.nn.softmax(top_v, axis=-1)
    flat_e = top_i.reshape(-1).astype(jnp.int32)
    flat_t = jnp.arange(T * TOP_K, dtype=jnp.int32) // TOP_K
    flat_g = gates.reshape(-1)
    order = jnp.argsort(flat_e)
    se = flat_e[order]
    counts = jnp.bincount(flat_e, length=N_EXPERTS).astype(jnp.int32)
    starts = jnp.cumsum(counts) - counts
    padded = ((counts + MOE_BLOCK - 1) // MOE_BLOCK) * MOE_BLOCK
    pends = jnp.cumsum(padded)
    pstarts = pends - padded
    dest = pstarts[se] + (jnp.arange(T * TOP_K, dtype=jnp.int32) - starts[se])
    P = T * TOP_K + N_EXPERTS * MOE_BLOCK
    nblk = P // MOE_BLOCK
    slot_tok = jnp.full((P,), T, dtype=jnp.int32).at[dest].set(flat_t[order])
    slot_g = jnp.zeros((P,), jnp.float32).at[dest].set(flat_g[order])
    blk_e = jnp.minimum(jnp.searchsorted(pends, jnp.arange(nblk, dtype=jnp.int32) * MOE_BLOCK, side='right'), N_EXPERTS - 1)
    xpad = jnp.concatenate([xt, jnp.zeros((1, D), xt.dtype)], axis=0)

    def expert_block(args):
        tok, e = args
        xb = xpad[tok]
        gu = xb @ w_gu[e] + b_gu[e]
        gate = jnp.minimum(gu[:, :D_FF], SWIGLU_LIMIT)
        up = jnp.clip(gu[:, D_FF:], -SWIGLU_LIMIT, SWIGLU_LIMIT)
        act = (up + 1.0) * gate * jax.nn.sigmoid(SWIGLU_ALPHA * gate)
        return act @ w_down[e] + b_down[e]

    yb = lax.map(expert_block, (slot_tok.reshape(nblk, MOE_BLOCK), blk_e))
    y = jnp.zeros((T + 1, D), h.dtype).at[slot_tok].add(yb.reshape(P, D) * slot_g[:, None].astype(h.dtype))
    return y[:T].reshape(B, S, D)


def _encoder_layer(x, c, w_ada, b_ada, w_in, sinks_a, q_norm_g, kv_norm_g, w_uq, w_ukv, w_a_out, w_b_out, w_o, ln1_g, ln1_b, w_router, b_router, w_gu, b_gu, w_down, b_down, ln2_g, ln2_b):
    B, S, _ = x.shape
    alpha = (2.0 * DEPTH) ** 0.25
    mod = (jax.nn.silu(c) @ w_ada + b_ada)[:, None, :]
    sh1, sc1, g1, sh2, sc2, g2 = jnp.split(mod, N_MOD, axis=-1)
    h = x * (1.0 + sc1) + sh1
    proj = h @ w_in
    parts = []
    off = 0
    for w in IN_WIDTHS:
        parts.append(proj[..., off:off + w])
        off += w
    qa, ka, va, cq, ckv, kr, ga, gb = parts
    oa = _window_gqa(qa.reshape(B, S, N_Q_A, HD_A), ka.reshape(B, S, N_KV_A, HD_A), va.reshape(B, S, N_KV_A, HD_A), sinks_a)
    pos = jnp.arange(S, dtype=jnp.float32)
    q = (_rms_norm(cq, q_norm_g) @ w_uq).reshape(B, S, N_H_B, NOPE_B + ROPE_B)
    kv = (_rms_norm(ckv, kv_norm_g) @ w_ukv).reshape(B, S, N_H_B, NOPE_B + V_B)
    q_rot = _rope(q[..., NOPE_B:], pos)
    k_rot = _rope(kr[:, :, None, :], pos)
    qb = jnp.concatenate([q[..., :NOPE_B], q_rot], axis=-1)
    kb = jnp.concatenate([kv[..., :NOPE_B], jnp.broadcast_to(k_rot, (B, S, N_H_B, ROPE_B))], axis=-1)
    vb = kv[..., NOPE_B:]
    ob = _dense_attn_blocks(qb, kb, vb)
    merged = jax.nn.sigmoid(ga) * (oa @ w_a_out) + jax.nn.sigmoid(gb) * (ob @ w_b_out)
    x = _layer_norm(alpha * x + g1 * (merged @ w_o), ln1_g, ln1_b)
    h2 = x * (1.0 + sc2) + sh2
    ffn = _moe(h2, w_router, b_router, w_gu, b_gu, w_down, b_down)
    x = _layer_norm(alpha * x + g2 * ffn, ln2_g, ln2_b)
    return x


def setup_inputs(seed: int = 0) -> dict:
    key = jax.random.key(seed)
    ks = jax.random.split(key, 26)
    f32 = jnp.float32
    beta = (8.0 * DEPTH) ** -0.25

    def nrm(k, shape, scale):
        return jax.random.normal(k, shape, f32) * scale

    L, D, E = DEPTH, D_MODEL, N_EXPERTS
    return {
        'x_prompt': nrm(ks[0], (BATCH, SEQ, D), 1.0),
        'x_sample': nrm(ks[1], (DEC_BATCH, DEC_SEQ, D), 1.0),
        'c_prompt': nrm(ks[2], (BATCH, D), 1.0),
        'c_sample': nrm(ks[3], (DEC_BATCH, D), 1.0),
        'w_ada': nrm(ks[4], (L, D, N_MOD * D), 0.5 * D ** -0.5),
        'b_ada': nrm(ks[5], (L, N_MOD * D), 0.02),
        'w_in': nrm(ks[6], (L, D, IN_W), D ** -0.5),
        'sinks_a': nrm(ks[7], (L, N_Q_A), 0.5),
        'q_norm_g': 1.0 + nrm(ks[8], (L, Q_LORA), 0.02),
        'kv_norm_g': 1.0 + nrm(ks[9], (L, KV_LORA), 0.02),
        'w_uq': nrm(ks[10], (L, Q_LORA, N_H_B * (NOPE_B + ROPE_B)), Q_LORA ** -0.5),
        'w_ukv': nrm(ks[11], (L, KV_LORA, N_H_B * (NOPE_B + V_B)), KV_LORA ** -0.5),
        'w_a_out': nrm(ks[12], (L, QA_W, D), QA_W ** -0.5),
        'w_b_out': nrm(ks[13], (L, N_H_B * V_B, D), (N_H_B * V_B) ** -0.5),
        'w_o': nrm(ks[14], (L, D, D), beta * D ** -0.5),
        'ln1_g': 1.0 + nrm(ks[15], (L, D), 0.02),
        'ln1_b': nrm(ks[16], (L, D), 0.02),
        'w_router': nrm(ks[17], (L, D, E), D ** -0.5),
        'b_router': nrm(ks[18], (L, E), 0.01),
        'w_gu': nrm(ks[19], (L, E, D, 2 * D_FF), D ** -0.5),
        'b_gu': nrm(ks[20], (L, E, 2 * D_FF), 0.02),
        'w_down': nrm(ks[21], (L, E, D_FF, D), beta * D_FF ** -0.5),
        'b_down': nrm(ks[22], (L, E, D), 0.02),
        'ln2_g': 1.0 + nrm(ks[23], (L, D), 0.02),
        'ln2_b': nrm(ks[24], (L, D), 0.02),
    }


def reference(x_prompt, x_sample, c_prompt, c_sample, w_ada, b_ada, w_in, sinks_a, q_norm_g, kv_norm_g, w_uq, w_ukv, w_a_out, w_b_out, w_o, ln1_g, ln1_b, w_router, b_router, w_gu, b_gu, w_down, b_down, ln2_g, ln2_b):
    y_prompt = x_prompt
    y_sample = x_sample
    for l in range(DEPTH):
        lp = (w_ada[l], b_ada[l], w_in[l], sinks_a[l], q_norm_g[l], kv_norm_g[l], w_uq[l], w_ukv[l], w_a_out[l], w_b_out[l], w_o[l], ln1_g[l], ln1_b[l], w_router[l], b_router[l], w_gu[l], b_gu[l], w_down[l], b_down[l], ln2_g[l], ln2_b[l])
        y_prompt = _encoder_layer(y_prompt, c_prompt, *lp)
        y_sample = _encoder_layer(y_sample, c_sample, *lp)
    return (y_prompt, y_sample)
```

```python
import functools

import jax
import jax.numpy as jnp
from jax import lax
from jax.experimental import pallas as pl
from jax.experimental.pallas import tpu as pltpu

N_Q_A = 16
N_KV_A = 2
HD_A = 64
WINDOW = 128
N_H_B = 16
NOPE_B = 128
ROPE_B = 64
V_B = 128
ROPE_THETA = 10000.0
TOP_K = 4
SWIGLU_LIMIT = 7.0
SWIGLU_ALPHA = 1.702
LN_EPS = 1e-5
RMS_EPS = 1e-6
N_MOD = 6

LANES_V7X = 128
VMEM_LIMIT_BYTES_V7X = 56 * 1024 * 1024

MXU_DTYPE = jnp.bfloat16
F32 = jnp.float32

TM_PROJ = 1024
TN_PROJ = 512
TM_LN = 256
TQ_MLA = 1024
TK_MLA = 1024
BM_MOE = 512
TF_MOE = 512
TM_FINAL = 128
TN_ADA = 1024


def _tile(n, pref):
    if n <= pref:
        return n
    t = pref - pref % LANES_V7X
    while t > 0 and n % t:
        t -= LANES_V7X
    assert t > 0, (n, pref)
    return t


def _params(*sem):
    return pltpu.CompilerParams(dimension_semantics=sem, vmem_limit_bytes=VMEM_LIMIT_BYTES_V7X)


def _sigmoid(x):
    return 1.0 / (1.0 + jnp.exp(-x))


def _rope_pairs(x, cos, sin):
    return x * cos + pltpu.roll(x, LANES_V7X // 2, axis=1) * sin


def _ada_kernel(c_ref, w_ref, b_ref, o_ref):
    c = c_ref[...]
    a = (c * _sigmoid(c)).astype(MXU_DTYPE)
    o_ref[...] = jnp.dot(a, w_ref[...].astype(MXU_DTYPE), preferred_element_type=F32) + b_ref[...]


def _ada(c, w_ada, b_ada):
    bp, d = c.shape
    n = w_ada.shape[1]
    tn = _tile(n, TN_ADA)
    return pl.pallas_call(
        _ada_kernel,
        grid=(n // tn,),
        in_specs=[
            pl.BlockSpec((bp, d), lambda j: (0, 0)),
            pl.BlockSpec((d, tn), lambda j: (0, j)),
            pl.BlockSpec((1, tn), lambda j: (0, j)),
        ],
        out_specs=pl.BlockSpec((bp, tn), lambda j: (0, j)),
        out_shape=jax.ShapeDtypeStruct((bp, n), F32),
        compiler_params=_params("parallel"),
        name="ada",
    )(c, w_ada, b_ada.reshape(1, n))


def _mod_kernel(x_ref, mod_ref, o_ref):
    sh = mod_ref[0, 0:1, :]
    sc = mod_ref[0, 1:2, :]
    o_ref[...] = (x_ref[...] * (1.0 + sc) + sh).astype(o_ref.dtype)


def _modulate(x2d, mod3, seq):
    t, d = x2d.shape
    tm = _tile(seq, TM_PROJ)
    per = seq // tm
    return pl.pallas_call(
        _mod_kernel,
        grid=(t // tm,),
        in_specs=[
            pl.BlockSpec((tm, d), lambda i: (i, 0)),
            pl.BlockSpec((1, N_MOD, d), lambda i: (i // per, 0, 0)),
        ],
        out_specs=pl.BlockSpec((tm, d), lambda i: (i, 0)),
        out_shape=jax.ShapeDtypeStruct((t, d), MXU_DTYPE),
        compiler_params=_params("parallel"),
        name="modulate1",
    )(x2d, mod3)


def _mm_kernel(x_ref, w_ref, o_ref):
    o_ref[...] = jnp.dot(x_ref[...], w_ref[...], preferred_element_type=F32).astype(o_ref.dtype)


def _matmul(x, w, out_dtype, name):
    m, k = x.shape
    n = w.shape[1]
    tm = _tile(m, TM_PROJ)
    tn = _tile(n, TN_PROJ)
    return pl.pallas_call(
        _mm_kernel,
        grid=(m // tm, n // tn),
        in_specs=[
            pl.BlockSpec((tm, k), lambda i, j: (i, 0)),
            pl.BlockSpec((k, tn), lambda i, j: (0, j)),
        ],
        out_specs=pl.BlockSpec((tm, tn), lambda i, j: (i, j)),
        out_shape=jax.ShapeDtypeStruct((m, n), out_dtype),
        compiler_params=_params("parallel", "parallel"),
        name=name,
    )(x, w)


def _mm_rope_kernel(x_ref, w_ref, cos_ref, sin_ref, o_ref):
    acc = jnp.dot(x_ref[...], w_ref[...], preferred_element_type=F32)
    o_ref[...] = _rope_pairs(acc, cos_ref[...], sin_ref[...]).astype(o_ref.dtype)


def _matmul_rope(x, w, cos, sin, seq, name):
    m, k = x.shape
    n = w.shape[1]
    assert n == LANES_V7X
    tm = _tile(seq, TM_PROJ)
    per = seq // tm
    return pl.pallas_call(
        _mm_rope_kernel,
        grid=(m // tm,),
        in_specs=[
            pl.BlockSpec((tm, k), lambda i: (i, 0)),
            pl.BlockSpec((k, n), lambda i: (0, 0)),
            pl.BlockSpec((tm, n), lambda i: (i % per, 0)),
            pl.BlockSpec((tm, n), lambda i: (i % per, 0)),
        ],
        out_specs=pl.BlockSpec((tm, n), lambda i: (i, 0)),
        out_shape=jax.ShapeDtypeStruct((m, n), MXU_DTYPE),
        compiler_params=_params("parallel"),
        name=name,
    )(x, w, cos, sin)


def _rms_mm_kernel(x_ref, g_ref, w_ref, *rest, rope, scale):
    if rope:
        cos_ref, sin_ref, o_ref, xn_ref = rest
    else:
        o_ref, xn_ref = rest

    @pl.when(pl.program_id(1) == 0)
    def _():
        xf = x_ref[...]
        r = lax.rsqrt(jnp.mean(xf * xf, axis=-1, keepdims=True) + RMS_EPS)
        xn_ref[...] = ((xf * r) * g_ref[...]).astype(xn_ref.dtype)

    acc = jnp.dot(xn_ref[...], w_ref[...], preferred_element_type=F32)
    if rope:
        acc = acc * scale
        cos = cos_ref[...]
        sin = sin_ref[...]
        head_w = NOPE_B + LANES_V7X
        pieces = []
        for c in range(acc.shape[1] // head_w):
            pieces.append(acc[:, c * head_w:c * head_w + NOPE_B])
            pieces.append(_rope_pairs(acc[:, c * head_w + NOPE_B:(c + 1) * head_w], cos, sin))
        acc = jnp.concatenate(pieces, axis=1)
    o_ref[...] = acc.astype(o_ref.dtype)


def _rms_matmul(lat, col_block, gain, w, seq, name, rope_tables=None, scale=1.0):
    m = lat.shape[0]
    k, n = w.shape
    tm = _tile(seq, TM_PROJ)
    rope = rope_tables is not None
    tn = _tile(n, max(TN_PROJ, NOPE_B + LANES_V7X) if rope else TN_PROJ)
    per = seq // tm
    in_specs = [
        pl.BlockSpec((tm, k), lambda i, j: (i, col_block)),
        pl.BlockSpec((1, k), lambda i, j: (0, 0)),
        pl.BlockSpec((k, tn), lambda i, j: (0, j)),
    ]
    args = [lat, gain.reshape(1, k), w]
    if rope:
        in_specs += [pl.BlockSpec((tm, LANES_V7X), lambda i, j: (i % per, 0))] * 2
        args += list(rope_tables)
    return pl.pallas_call(
        functools.partial(_rms_mm_kernel, rope=rope, scale=scale),
        grid=(m // tm, n // tn),
        in_specs=in_specs,
        out_specs=pl.BlockSpec((tm, tn), lambda i, j: (i, j)),
        out_shape=jax.ShapeDtypeStruct((m, n), MXU_DTYPE),
        scratch_shapes=[pltpu.VMEM((tm, k), MXU_DTYPE)],
        compiler_params=_params("parallel", "arbitrary"),
        name=name,
    )(*args)


def _window_kernel(q_ref, kp_ref, ko_ref, kn_ref, vp_ref, vo_ref, vn_ref, sink_ref, o_ref, *, seq, slopes):
    n = pl.program_id(1)
    w = WINDOW
    group = N_Q_A // N_KV_A
    scale = HD_A ** -0.5
    q = q_ref[...]
    k3 = jnp.concatenate([kp_ref[...], ko_ref[...], kn_ref[...]], axis=0)
    v3 = jnp.concatenate([vp_ref[...], vo_ref[...], vn_ref[...]], axis=0)
    lane = lax.broadcasted_iota(jnp.int32, k3.shape, 1)
    lo = lane < HD_A
    zero = jnp.zeros_like(k3)
    kbd = jnp.concatenate([jnp.where(lo, k3, zero), jnp.where(lo, zero, k3)], axis=0)
    vbd = jnp.concatenate([jnp.where(lo, v3, zero), jnp.where(lo, zero, v3)], axis=0)

    row = lax.broadcasted_iota(jnp.int32, (w, 6 * w), 0)
    col = lax.broadcasted_iota(jnp.int32, (w, 6 * w), 1)
    first = col < 3 * w
    krel = jnp.where(first, col, col - 3 * w) - w
    dist = jnp.abs(row - krel)
    kabs = n * w + krel
    valid = (dist <= w) & (kabs >= 0) & (kabs < seq)
    distf = dist.astype(F32)
    out_lo = lax.broadcasted_iota(jnp.int32, (w, 2 * HD_A), 1) < HD_A

    outs = []
    for g in range(group):
        qg = q[:, g * 2 * HD_A:(g + 1) * 2 * HD_A]
        s = lax.dot_general(qg, kbd, (((1,), (1,)), ((), ())), preferred_element_type=F32) * scale
        slope = jnp.where(first, slopes[g], slopes[group + g])
        s = jnp.where(valid, s - slope * distf, -jnp.inf)
        sink0 = sink_ref[0:1, g:g + 1]
        sink1 = sink_ref[0:1, group + g:group + g + 1]
        m0 = jnp.maximum(jnp.max(s[:, :3 * w], axis=1, keepdims=True), sink0)
        m1 = jnp.maximum(jnp.max(s[:, 3 * w:], axis=1, keepdims=True), sink1)
        p = jnp.exp(s - jnp.where(first, m0, m1))
        d0 = jnp.sum(p[:, :3 * w], axis=1, keepdims=True) + jnp.exp(sink0 - m0)
        d1 = jnp.sum(p[:, 3 * w:], axis=1, keepdims=True) + jnp.exp(sink1 - m1)
        o = jnp.dot(p.astype(MXU_DTYPE), vbd, preferred_element_type=F32)
        outs.append(o * jnp.where(out_lo, 1.0 / d0, 1.0 / d1))
    o_ref[...] = jnp.concatenate(outs, axis=1).astype(o_ref.dtype)


def _window_attention(qkv, sinks, batch, seq):
    t = qkv.shape[0]
    w = WINDOW
    nb = seq // w
    qw = N_Q_A * HD_A
    kvw = N_KV_A * HD_A
    assert N_KV_A == 2 and kvw == LANES_V7X
    kblk = qw // kvw
    vblk = kblk + 1
    slopes = tuple(2.0 ** (-8.0 * (i + 1) / N_Q_A) for i in range(N_Q_A))

    def kv_spec(col, off):
        return pl.BlockSpec((w, kvw), lambda b, n: (b * nb + jnp.clip(n + off, 0, nb - 1), col))

    return pl.pallas_call(
        functools.partial(_window_kernel, seq=seq, slopes=slopes),
        grid=(batch, nb),
        in_specs=[
            pl.BlockSpec((w, qw), lambda b, n: (b * nb + n, 0)),
            kv_spec(kblk, -1), kv_spec(kblk, 0), kv_spec(kblk, 1),
            kv_spec(vblk, -1), kv_spec(vblk, 0), kv_spec(vblk, 1),
            pl.BlockSpec((1, N_Q_A), lambda b, n: (0, 0)),
        ],
        out_specs=pl.BlockSpec((w, qw), lambda b, n: (b * nb + n, 0)),
        out_shape=jax.ShapeDtypeStruct((t, qw), MXU_DTYPE),
        compiler_params=_params("parallel", "parallel"),
        name="window_attention",
    )(qkv, qkv, qkv, qkv, qkv, qkv, qkv, sinks.reshape(1, N_Q_A))


def _mla_kernel(q_ref, kn_ref, kr_ref, v_ref, o_ref, m_ref, l_ref, acc_ref):
    ki = pl.program_id(3)

    @pl.when(ki == 0)
    def _():
        m_ref[...] = jnp.full(m_ref.shape, -jnp.inf, F32)
        l_ref[...] = jnp.zeros(l_ref.shape, F32)
        acc_ref[...] = jnp.zeros(acc_ref.shape, F32)

    k = jnp.concatenate([kn_ref[...], kr_ref[...]], axis=1)
    s = lax.dot_general(q_ref[...], k, (((1,), (1,)), ((), ())), preferred_element_type=F32)
    m_old = m_ref[...]
    m_new = jnp.maximum(m_old, jnp.max(s, axis=1, keepdims=True))
    alpha = jnp.exp(m_old - m_new)
    p = jnp.exp(s - m_new)
    l_ref[...] = alpha * l_ref[...] + jnp.sum(p, axis=1, keepdims=True)
    acc_ref[...] = alpha * acc_ref[...] + jnp.dot(p.astype(MXU_DTYPE), v_ref[...], preferred_element_type=F32)
    m_ref[...] = m_new

    @pl.when(ki == pl.num_programs(3) - 1)
    def _():
        o_ref[...] = (acc_ref[...] / l_ref[...]).astype(o_ref.dtype)


def _mla_attention(q, kv, kr, batch, seq):
    t = q.shape[0]
    tq = _tile(seq, TQ_MLA)
    tk = _tile(seq, TK_MLA)
    nq = seq // tq
    nk = seq // tk
    qh = NOPE_B + LANES_V7X
    assert NOPE_B == LANES_V7X and V_B == LANES_V7X
    return pl.pallas_call(
        _mla_kernel,
        grid=(batch, N_H_B, nq, nk),
        in_specs=[
            pl.BlockSpec((tq, qh), lambda b, h, qi, ki: (b * nq + qi, h)),
            pl.BlockSpec((tk, NOPE_B), lambda b, h, qi, ki: (b * nk + ki, 2 * h)),
            pl.BlockSpec((tk, LANES_V7X), lambda b, h, qi, ki: (b * nk + ki, 0)),
            pl.BlockSpec((tk, V_B), lambda b, h, qi, ki: (b * nk + ki, 2 * h + 1)),
        ],
        out_specs=pl.BlockSpec((tq, V_B), lambda b, h, qi, ki: (b * nq + qi, h)),
        out_shape=jax.ShapeDtypeStruct((t, N_H_B * V_B), MXU_DTYPE),
        scratch_shapes=[
            pltpu.VMEM((tq, 1), F32),
            pltpu.VMEM((tq, 1), F32),
            pltpu.VMEM((tq, V_B), F32),
        ],
        compiler_params=_params("parallel", "parallel", "parallel", "arbitrary"),
        name="mla_attention",
    )(q, kv, kr, kv)


def _merge_kernel(oa_ref, ob_ref, wa_ref, wb_ref, ga_ref, gb_ref, o_ref):
    a = jnp.dot(oa_ref[...], wa_ref[...], preferred_element_type=F32)
    b = jnp.dot(ob_ref[...], wb_ref[...], preferred_element_type=F32)
    o_ref[...] = (_sigmoid(ga_ref[...]) * a + _sigmoid(gb_ref[...]) * b).astype(o_ref.dtype)


def _merge(oa, ob, wa, wb, gates):
    t = oa.shape[0]
    d = wa.shape[1]
    tm = _tile(t, TM_PROJ)
    tn = _tile(d, TN_PROJ)
    nj = d // tn
    return pl.pallas_call(
        _merge_kernel,
        grid=(t // tm, nj),
        in_specs=[
            pl.BlockSpec((tm, oa.shape[1]), lambda i, j: (i, 0)),
            pl.BlockSpec((tm, ob.shape[1]), lambda i, j: (i, 0)),
            pl.BlockSpec((wa.shape[0], tn), lambda i, j: (0, j)),
            pl.BlockSpec((wb.shape[0], tn), lambda i, j: (0, j)),
            pl.BlockSpec((tm, tn), lambda i, j: (i, j)),
            pl.BlockSpec((tm, tn), lambda i, j: (i, nj + j)),
        ],
        out_specs=pl.BlockSpec((tm, tn), lambda i, j: (i, j)),
        out_shape=jax.ShapeDtypeStruct((t, d), MXU_DTYPE),
        compiler_params=_params("parallel", "parallel"),
        name="merge",
    )(oa, ob, wa, wb, gates, gates)


def _layer_norm_rows(z, g, b):
    mu = jnp.mean(z, axis=-1, keepdims=True)
    zc = z - mu
    var = jnp.mean(zc * zc, axis=-1, keepdims=True)
    return (zc * lax.rsqrt(var + LN_EPS)) * g + b


def _ln1_kernel(mg_ref, wo_ref, x_ref, mod_ref, lng_ref, lnb_ref, wr_ref, br_ref,
                x1_ref, h2_ref, ri_ref, rg_ref, *, alpha):
    y = jnp.dot(mg_ref[...], wo_ref[...], preferred_element_type=F32)
    g1 = mod_ref[0, 2:3, :]
    sh2 = mod_ref[0, 3:4, :]
    sc2 = mod_ref[0, 4:5, :]
    x1 = _layer_norm_rows(alpha * x_ref[...] + g1 * y, lng_ref[...], lnb_ref[...])
    x1_ref[...] = x1
    h2 = x1 * (1.0 + sc2) + sh2
    h2_ref[...] = h2
    logits = jnp.dot(h2, wr_ref[...], preferred_element_type=F32, precision=lax.Precision.HIGHEST) + br_ref[...]
    ne = logits.shape[1]
    eio = lax.broadcasted_iota(jnp.int32, logits.shape, 1).astype(F32)
    vals, idxs = [], []
    for _ in range(TOP_K):
        mx = jnp.max(logits, axis=1, keepdims=True)
        ix = jnp.min(jnp.where(logits == mx, eio, float(ne)), axis=1, keepdims=True)
        vals.append(mx)
        idxs.append(ix)
        logits = jnp.where(eio == ix, -jnp.inf, logits)
    es = [jnp.exp(v - vals[0]) for v in vals]
    tot = es[0]
    for e in es[1:]:
        tot = tot + e
    lane = lax.broadcasted_iota(jnp.int32, ri_ref.shape, 1)
    ri = jnp.zeros(ri_ref.shape, F32)
    rg = jnp.zeros(rg_ref.shape, F32)
    for k in range(TOP_K):
        ri = jnp.where(lane == k, idxs[k], ri)
        rg = jnp.where(lane == k, es[k] / tot, rg)
    ri_ref[...] = ri.astype(jnp.int32)
    rg_ref[...] = rg


def _oproj_ln1_router(merged, w_o, x2d, mod3, ln_g, ln_b, w_router, b_router, seq, alpha):
    t, d = x2d.shape
    ne = w_router.shape[1]
    tm = _tile(seq, TM_LN)
    per = seq // tm
    row = lambda i: (i, 0)
    const = lambda i: (0, 0)
    return pl.pallas_call(
        functools.partial(_ln1_kernel, alpha=alpha),
        grid=(t // tm,),
        in_specs=[
            pl.BlockSpec((tm, d), row),
            pl.BlockSpec((d, d), const),
            pl.BlockSpec((tm, d), row),
            pl.BlockSpec((1, N_MOD, d), lambda i: (i // per, 0, 0)),
            pl.BlockSpec((1, d), const),
            pl.BlockSpec((1, d), const),
            pl.BlockSpec((d, ne), const),
            pl.BlockSpec((1, ne), const),
        ],
        out_specs=[
            pl.BlockSpec((tm, d), row),
            pl.BlockSpec((tm, d), row),
            pl.BlockSpec((tm, LANES_V7X), row),
            pl.BlockSpec((tm, LANES_V7X), row),
        ],
        out_shape=[
            jax.ShapeDtypeStruct((t, d), F32),
            jax.ShapeDtypeStruct((t, d), F32),
            jax.ShapeDtypeStruct((t, LANES_V7X), jnp.int32),
            jax.ShapeDtypeStruct((t, LANES_V7X), F32),
        ],
        compiler_params=_params("parallel"),
        name="oproj_ln1_router",
    )(merged, w_o, x2d, mod3, ln_g.reshape(1, d), ln_b.reshape(1, d), w_router, b_router.reshape(1, ne))


def _moe_kernel(be_ref, used_ref, tok_ref, g_ref, h_hbm, wg_ref, wu_ref, bg_ref, bu_ref, wd_ref, bd_ref,
                o_ref, xg_ref, xb_ref, acc_ref, sem):
    del be_ref
    i = pl.program_id(0)
    f = pl.program_id(1)
    bm = xg_ref.shape[0]
    used = used_ref[i] > 0

    def gather_row(r):
        return pltpu.make_async_copy(h_hbm.at[pl.ds(tok_ref[0, 0, r], 1), :], xg_ref.at[pl.ds(r, 1), :], sem.at[0])

    @pl.when(jnp.logical_and(f == 0, used))
    def _():
        def issue(r, carry):
            gather_row(r).start()
            return carry

        lax.fori_loop(0, bm, issue, 0)
        pltpu.make_async_copy(h_hbm.at[pl.ds(0, bm), :], xg_ref, sem.at[0]).wait()
        xb_ref[...] = xg_ref[...].astype(xb_ref.dtype)
        acc_ref[...] = jnp.zeros(acc_ref.shape, F32)

    @pl.when(used)
    def _():
        x = xb_ref[...]
        gate = jnp.dot(x, wg_ref[0], preferred_element_type=F32) + bg_ref[0]
        up = jnp.dot(x, wu_ref[0], preferred_element_type=F32) + bu_ref[0]
        gate = jnp.minimum(gate, SWIGLU_LIMIT)
        up = jnp.clip(up, -SWIGLU_LIMIT, SWIGLU_LIMIT)
        act = (up + 1.0) * gate * _sigmoid(SWIGLU_ALPHA * gate)
        acc_ref[...] += jnp.dot(act.astype(MXU_DTYPE), wd_ref[0], preferred_element_type=F32)

    last = f == pl.num_programs(1) - 1

    @pl.when(jnp.logical_and(last, used))
    def _():
        o_ref[...] = (acc_ref[...] + bd_ref[0]) * g_ref[...]

    @pl.when(jnp.logical_and(last, jnp.logical_not(used)))
    def _():
        o_ref[...] = jnp.zeros(o_ref.shape, F32)


def _moe(h2, blk_e, blk_used, slot_tok, slot_g, w_gu, b_gu, w_down, b_down, bm):
    t, d = h2.shape
    ne, _, two_ff = w_gu.shape
    dff = two_ff // 2
    tf = _tile(dff, TF_MOE)
    nf = dff // tf
    p = slot_tok.shape[0]
    nblk = p // bm
    grid_spec = pltpu.PrefetchScalarGridSpec(
        num_scalar_prefetch=2,
        grid=(nblk, nf),
        in_specs=[
            pl.BlockSpec((1, 1, bm), lambda i, f, be, bu: (i, 0, 0), memory_space=pltpu.SMEM),
            pl.BlockSpec((bm, 1), lambda i, f, be, bu: (i, 0)),
            pl.BlockSpec(memory_space=pl.ANY),
            pl.BlockSpec((1, d, tf), lambda i, f, be, bu: (be[i], 0, f)),
            pl.BlockSpec((1, d, tf), lambda i, f, be, bu: (be[i], 0, nf + f)),
            pl.BlockSpec((1, 1, tf), lambda i, f, be, bu: (be[i], 0, f)),
            pl.BlockSpec((1, 1, tf), lambda i, f, be, bu: (be[i], 0, nf + f)),
            pl.BlockSpec((1, tf, d), lambda i, f, be, bu: (be[i], f, 0)),
            pl.BlockSpec((1, 1, d), lambda i, f, be, bu: (be[i], 0, 0)),
        ],
        out_specs=pl.BlockSpec((bm, d), lambda i, f, be, bu: (i, 0)),
        scratch_shapes=[
            pltpu.VMEM((bm, d), F32),
            pltpu.VMEM((bm, d), MXU_DTYPE),
            pltpu.VMEM((bm, d), F32),
            pltpu.SemaphoreType.DMA((1,)),
        ],
    )
    return pl.pallas_call(
        _moe_kernel,
        grid_spec=grid_spec,
        out_shape=jax.ShapeDtypeStruct((p, d), F32),
        compiler_params=_params("arbitrary", "arbitrary"),
        name="moe_experts",
    )(blk_e, blk_used, slot_tok.reshape(nblk, 1, bm), slot_g.reshape(p, 1), h2,
      w_gu, w_gu, b_gu.reshape(ne, 1, two_ff), b_gu.reshape(ne, 1, two_ff), w_down, b_down.reshape(ne, 1, d))


def _final_kernel(dst_ref, x1_ref, mod_ref, lng_ref, lnb_ref, yb_hbm, o_ref, buf_ref, sem, *, alpha):
    tm = x1_ref.shape[0]
    nrow = buf_ref.shape[0]

    def gather_row(r):
        return pltpu.make_async_copy(yb_hbm.at[pl.ds(dst_ref[0, 0, r], 1), :], buf_ref.at[pl.ds(r, 1), :], sem.at[0])

    def issue(r, carry):
        gather_row(r).start()
        return carry

    lax.fori_loop(0, nrow, issue, 0)
    pltpu.make_async_copy(yb_hbm.at[pl.ds(0, nrow), :], buf_ref, sem.at[0]).wait()
    y = buf_ref[0:tm, :]
    for k in range(1, TOP_K):
        y = y + buf_ref[k * tm:(k + 1) * tm, :]
    g2 = mod_ref[0, 5:6, :]
    o_ref[...] = _layer_norm_rows(alpha * x1_ref[...] + g2 * y, lng_ref[...], lnb_ref[...])


def _combine_ln2(dst, x1, mod3, ln_g, ln_b, yb, seq, alpha):
    t, d = x1.shape
    tm = _tile(seq, TM_FINAL)
    per = seq // tm
    nt = t // tm
    dst_tiles = dst.reshape(nt, tm, TOP_K).transpose(0, 2, 1).reshape(nt, 1, TOP_K * tm)
    return pl.pallas_call(
        functools.partial(_final_kernel, alpha=alpha),
        grid=(nt,),
        in_specs=[
            pl.BlockSpec((1, 1, TOP_K * tm), lambda i: (i, 0, 0), memory_space=pltpu.SMEM),
            pl.BlockSpec((tm, d), lambda i: (i, 0)),
            pl.BlockSpec((1, N_MOD, d), lambda i: (i // per, 0, 0)),
            pl.BlockSpec((1, d), lambda i: (0, 0)),
            pl.BlockSpec((1, d), lambda i: (0, 0)),
            pl.BlockSpec(memory_space=pl.ANY),
        ],
        out_specs=pl.BlockSpec((tm, d), lambda i: (i, 0)),
        out_shape=jax.ShapeDtypeStruct((t, d), F32),
        scratch_shapes=[
            pltpu.VMEM((TOP_K * tm, d), F32),
            pltpu.SemaphoreType.DMA((1,)),
        ],
        compiler_params=_params("arbitrary"),
        name="combine_ln2",
    )(dst_tiles, x1, mod3, ln_g.reshape(1, d), ln_b.reshape(1, d), yb)


def _route_slots(top_i, gates, n_experts, bm):
    t = top_i.shape[0]
    na = t * TOP_K
    flat_e = top_i.reshape(-1)
    order = jnp.argsort(flat_e)
    se = flat_e[order]
    counts = jnp.bincount(flat_e, length=n_experts).astype(jnp.int32)
    starts = jnp.cumsum(counts) - counts
    padded = ((counts + bm - 1) // bm) * bm
    pends = jnp.cumsum(padded)
    pstarts = pends - padded
    dest_sorted = pstarts[se] + (jnp.arange(na, dtype=jnp.int32) - starts[se])
    p = na + n_experts * bm
    nblk = p // bm
    slot_tok = jnp.zeros((p,), jnp.int32).at[dest_sorted].set((order // TOP_K).astype(jnp.int32))
    slot_g = jnp.zeros((p,), F32).at[dest_sorted].set(gates.reshape(-1)[order])
    dest = jnp.zeros((na,), jnp.int32).at[order].set(dest_sorted).reshape(t, TOP_K)
    blk_start = jnp.arange(nblk, dtype=jnp.int32) * bm
    blk_e = jnp.minimum(jnp.searchsorted(pends, blk_start, side='right'), n_experts - 1).astype(jnp.int32)
    blk_used = (blk_start < pends[-1]).astype(jnp.int32)
    return slot_tok, slot_g, dest, blk_e, blk_used


def _pad_rope_cols(w):
    half = ROPE_B // 2
    z = jnp.zeros(w.shape[:-1] + (LANES_V7X // 2 - half,), w.dtype)
    return jnp.concatenate([w[..., :half], z, w[..., half:], z], axis=-1)


def _rope_tables(seq):
    half = ROPE_B // 2
    pos = jnp.arange(seq, dtype=F32)
    freqs = ROPE_THETA ** (-jnp.arange(half, dtype=F32) / half)
    ang = pos[:, None] * freqs[None, :]
    cos = jnp.cos(ang)
    sin = jnp.sin(ang)
    z = jnp.zeros((seq, LANES_V7X // 2 - half), F32)
    return (jnp.concatenate([cos, z, cos, z], axis=1), jnp.concatenate([-sin, z, sin, z], axis=1))


def _prepare_layer(w_in, w_uq, w_ukv, w_a_out, w_b_out, w_o, w_gu, w_down):
    d = w_in.shape[0]
    qa_w = N_Q_A * HD_A
    ka_w = N_KV_A * HD_A
    q_lora = w_uq.shape[0]
    kv_lora = w_ukv.shape[0]
    assert q_lora == kv_lora
    group = N_Q_A // N_KV_A
    offs = [0]
    for wdt in (qa_w, ka_w, ka_w, q_lora, kv_lora, ROPE_B, d, d):
        offs.append(offs[-1] + wdt)
    assert offs[-1] == w_in.shape[1]
    part = lambda i: w_in[:, offs[i]:offs[i + 1]]
    w_qa = part(0).reshape(d, N_KV_A, group, HD_A).transpose(0, 2, 1, 3).reshape(d, qa_w)
    cast = lambda w: w.astype(MXU_DTYPE)
    w_uq3 = w_uq.reshape(q_lora, N_H_B, NOPE_B + ROPE_B)
    w_uq_p = jnp.concatenate([w_uq3[..., :NOPE_B], _pad_rope_cols(w_uq3[..., NOPE_B:])], axis=-1)
    return dict(
        w_qkv=cast(jnp.concatenate([w_qa, part(1), part(2)], axis=1)),
        w_lat=cast(jnp.concatenate([part(3), part(4)], axis=1)),
        w_kr=cast(_pad_rope_cols(part(5))),
        w_gates=cast(jnp.concatenate([part(6), part(7)], axis=1)),
        w_uq=cast(w_uq_p.reshape(q_lora, N_H_B * (NOPE_B + LANES_V7X))),
        w_ukv=cast(w_ukv),
        w_a_out=cast(w_a_out.reshape(N_KV_A, group, HD_A, d).transpose(1, 0, 2, 3).reshape(qa_w, d)),
        w_b_out=cast(w_b_out),
        w_o=cast(w_o),
        w_gu=cast(w_gu),
        w_down=cast(w_down),
    )


def _encoder_layer(x, mod, wts, sinks_a, q_norm_g, kv_norm_g, ln1_g, ln1_b, w_router, b_router,
                   b_gu, b_down, ln2_g, ln2_b, alpha):
    b, s, d = x.shape
    t = b * s
    x2d = x.reshape(t, d)
    mod3 = mod.reshape(b, N_MOD, d)
    tables = _rope_tables(s)

    h = _modulate(x2d, mod3, s)
    qkv = _matmul(h, wts['w_qkv'], MXU_DTYPE, "proj_qkv_a")
    lat = _matmul(h, wts['w_lat'], F32, "proj_latents")
    gates = _matmul(h, wts['w_gates'], F32, "proj_gates")
    kr = _matmul_rope(h, wts['w_kr'], tables[0], tables[1], s, "proj_k_rope")

    oa = _window_attention(qkv, sinks_a, b, s)

    qscale = (NOPE_B + ROPE_B) ** -0.5
    q = _rms_matmul(lat, 0, q_norm_g, wts['w_uq'], s, "q_up", rope_tables=tables, scale=qscale)
    kv = _rms_matmul(lat, 1, kv_norm_g, wts['w_ukv'], s, "kv_up")
    ob = _mla_attention(q, kv, kr, b, s)

    merged = _merge(oa, ob, wts['w_a_out'], wts['w_b_out'], gates)
    x1, h2, ridx, rgate = _oproj_ln1_router(merged, wts['w_o'], x2d, mod3, ln1_g, ln1_b, w_router, b_router, s, alpha)

    n_experts = w_router.shape[1]
    bm = _tile(t, BM_MOE)
    slot_tok, slot_g, dest, blk_e, blk_used = _route_slots(ridx[:, :TOP_K], rgate[:, :TOP_K], n_experts, bm)
    yb = _moe(h2, blk_e, blk_used, slot_tok, slot_g, wts['w_gu'], b_gu, wts['w_down'], b_down, bm)
    out = _combine_ln2(dest, x1, mod3, ln2_g, ln2_b, yb, s, alpha)
    return out.reshape(b, s, d)


def kernel(x_prompt, x_sample, c_prompt, c_sample, w_ada, b_ada, w_in, sinks_a, q_norm_g, kv_norm_g, w_uq, w_ukv, w_a_out, w_b_out, w_o, ln1_g, ln1_b, w_router, b_router, w_gu, b_gu, w_down, b_down, ln2_g, ln2_b):
    depth = w_ada.shape[0]
    alpha = (2.0 * depth) ** 0.25
    nbp = c_prompt.shape[0]
    nbs = c_sample.shape[0]
    sub = 8
    pad = (-(nbp + nbs)) % sub
    y_prompt, y_sample = x_prompt, x_sample
    for l in range(depth):
        c_all = jnp.concatenate([c_prompt, c_sample, jnp.zeros((pad, c_prompt.shape[1]), F32)], axis=0)
        mod = _ada(c_all, w_ada[l], b_ada[l])
        wts = _prepare_layer(w_in[l], w_uq[l], w_ukv[l], w_a_out[l], w_b_out[l], w_o[l], w_gu[l], w_down[l])
        rest = (wts, sinks_a[l], q_norm_g[l], kv_norm_g[l], ln1_g[l], ln1_b[l], w_router[l], b_router[l],
                b_gu[l], b_down[l], ln2_g[l], ln2_b[l], alpha)
        y_prompt = _encoder_layer(y_prompt, mod[:nbp], *rest)
        y_sample = _encoder_layer(y_sample, mod[nbp:nbp + nbs], *rest)
    return (y_prompt, y_sample)
```

```python
import functools

import jax
import jax.numpy as jnp
from jax import lax
from jax.experimental import pallas as pl
from jax.experimental.pallas import tpu as pltpu

N_Q_A = 16
N_KV_A = 2
HD_A = 64
WINDOW = 128
N_H_B = 16
NOPE_B = 128
ROPE_B = 64
V_B = 128
ROPE_THETA = 10000.0
TOP_K = 4
SWIGLU_LIMIT = 7.0
SWIGLU_ALPHA = 1.702
LN_EPS = 1e-5
RMS_EPS = 1e-6
N_MOD = 6
LOG2_E = 1.4426950408889634

LANES_V7X = 128
SUBLANES_V7X = 8
VMEM_LIMIT_BYTES_V7X = 56 * 1024 * 1024

MXU_DTYPE = jnp.bfloat16
F32 = jnp.float32

TM_PROJ = 1024
TN_PROJ = 512
TM_LN = 256
TQ_MLA = 512
TKC_MLA = 512
BM_MOE = 512
TF_MOE = 512
TM_FINAL = 128
TN_ADA = 1024


def _tile(n, pref):
    if n <= pref:
        return n
    align = LANES_V7X if pref >= LANES_V7X else SUBLANES_V7X
    t = pref - pref % align
    while t > 0 and n % t:
        t -= align
    assert t > 0, (n, pref)
    return t


def _params(*sem):
    return pltpu.CompilerParams(dimension_semantics=sem, vmem_limit_bytes=VMEM_LIMIT_BYTES_V7X)


def _sigmoid(x):
    return 1.0 / (1.0 + jnp.exp(-x))


def _rope_pairs(x, cos, sin):
    return x * cos + pltpu.roll(x, LANES_V7X // 2, axis=1) * sin


def _ada_kernel(c_ref, w_ref, b_ref, o_ref):
    c = c_ref[...]
    a = (c * _sigmoid(c)).astype(MXU_DTYPE)
    o_ref[...] = jnp.dot(a, w_ref[...].astype(MXU_DTYPE), preferred_element_type=F32) + b_ref[...]


def _ada(c, w_ada, b_ada):
    bp, d = c.shape
    n = w_ada.shape[1]
    tn = _tile(n, TN_ADA)
    return pl.pallas_call(
        _ada_kernel,
        grid=(n // tn,),
        in_specs=[
            pl.BlockSpec((bp, d), lambda j: (0, 0)),
            pl.BlockSpec((d, tn), lambda j: (0, j)),
            pl.BlockSpec((1, tn), lambda j: (0, j)),
        ],
        out_specs=pl.BlockSpec((bp, tn), lambda j: (0, j)),
        out_shape=jax.ShapeDtypeStruct((bp, n), F32),
        compiler_params=_params("parallel"),
        name="ada",
    )(c, w_ada, b_ada.reshape(1, n))


def _mod_kernel(x_ref, mod_ref, o_ref):
    sh = mod_ref[0, 0:1, :]
    sc = mod_ref[0, 1:2, :]
    o_ref[...] = (x_ref[...] * (1.0 + sc) + sh).astype(o_ref.dtype)


def _modulate(x2d, mod3, seq):
    t, d = x2d.shape
    tm = _tile(seq, TM_PROJ)
    per = seq // tm
    return pl.pallas_call(
        _mod_kernel,
        grid=(t // tm,),
        in_specs=[
            pl.BlockSpec((tm, d), lambda i: (i, 0)),
            pl.BlockSpec((1, N_MOD, d), lambda i: (i // per, 0, 0)),
        ],
        out_specs=pl.BlockSpec((tm, d), lambda i: (i, 0)),
        out_shape=jax.ShapeDtypeStruct((t, d), MXU_DTYPE),
        compiler_params=_params("parallel"),
        name="modulate1",
    )(x2d, mod3)


def _mm_kernel(x_ref, w_ref, o_ref):
    o_ref[...] = jnp.dot(x_ref[...], w_ref[...], preferred_element_type=F32).astype(o_ref.dtype)


def _matmul(x, w, out_dtype, name):
    m, k = x.shape
    n = w.shape[1]
    tm = _tile(m, TM_PROJ)
    tn = _tile(n, TN_PROJ)
    return pl.pallas_call(
        _mm_kernel,
        grid=(m // tm, n // tn),
        in_specs=[
            pl.BlockSpec((tm, k), lambda i, j: (i, 0)),
            pl.BlockSpec((k, tn), lambda i, j: (0, j)),
        ],
        out_specs=pl.BlockSpec((tm, tn), lambda i, j: (i, j)),
        out_shape=jax.ShapeDtypeStruct((m, n), out_dtype),
        compiler_params=_params("parallel", "parallel"),
        name=name,
    )(x, w)


def _mm_rope_kernel(x_ref, w_ref, cos_ref, sin_ref, o_ref):
    acc = jnp.dot(x_ref[...], w_ref[...], preferred_element_type=F32)
    o_ref[...] = _rope_pairs(acc, cos_ref[...], sin_ref[...]).astype(o_ref.dtype)


def _matmul_rope(x, w, cos, sin, seq, name):
    m, k = x.shape
    n = w.shape[1]
    assert n == LANES_V7X
    tm = _tile(seq, TM_PROJ)
    per = seq // tm
    return pl.pallas_call(
        _mm_rope_kernel,
        grid=(m // tm,),
        in_specs=[
            pl.BlockSpec((tm, k), lambda i: (i, 0)),
            pl.BlockSpec((k, n), lambda i: (0, 0)),
            pl.BlockSpec((tm, n), lambda i: (i % per, 0)),
            pl.BlockSpec((tm, n), lambda i: (i % per, 0)),
        ],
        out_specs=pl.BlockSpec((tm, n), lambda i: (i, 0)),
        out_shape=jax.ShapeDtypeStruct((m, n), MXU_DTYPE),
        compiler_params=_params("parallel"),
        name=name,
    )(x, w, cos, sin)


def _rms_norm_to(xn_ref, x_ref, g_ref):
    @pl.when(pl.program_id(1) == 0)
    def _():
        xf = x_ref[...]
        r = lax.rsqrt(jnp.mean(xf * xf, axis=-1, keepdims=True) + RMS_EPS)
        xn_ref[...] = ((xf * r) * g_ref[...]).astype(xn_ref.dtype)


def _q_up_kernel(x_ref, g_ref, w_ref, cos_ref, sin_ref, o_ref, xn_ref, *, scale):
    _rms_norm_to(xn_ref, x_ref, g_ref)
    acc = jnp.dot(xn_ref[...], w_ref[...], preferred_element_type=F32) * scale
    cos = cos_ref[...]
    sin = sin_ref[...]
    head_w = NOPE_B + LANES_V7X
    pieces = []
    for c in range(acc.shape[1] // head_w):
        pieces.append(acc[:, c * head_w:c * head_w + NOPE_B])
        pieces.append(_rope_pairs(acc[:, c * head_w + NOPE_B:(c + 1) * head_w], cos, sin))
    o_ref[...] = jnp.concatenate(pieces, axis=1).astype(o_ref.dtype)


def _k_up_kernel(x_ref, g_ref, w_ref, kr_ref, o_ref, xn_ref):
    _rms_norm_to(xn_ref, x_ref, g_ref)
    acc = jnp.dot(xn_ref[...], w_ref[...], preferred_element_type=F32).astype(o_ref.dtype)
    kr = kr_ref[...]
    pieces = []
    for c in range(acc.shape[1] // NOPE_B):
        pieces.append(acc[:, c * NOPE_B:(c + 1) * NOPE_B])
        pieces.append(kr)
    o_ref[...] = jnp.concatenate(pieces, axis=1)


def _v_up_t_kernel(x_ref, g_ref, wt_ref, o_ref, xn_ref):
    _rms_norm_to(xn_ref, x_ref, g_ref)
    acc = lax.dot_general(wt_ref[...], xn_ref[...], (((1,), (1,)), ((), ())), preferred_element_type=F32)
    o_ref[0] = acc.reshape(o_ref.shape[1:]).astype(o_ref.dtype)


def _latent_up(kernel_fn, lat, col_block, gain, w, extra, extra_specs, out_spec, out_shape, tm, grid_n, w_spec, name):
    m = lat.shape[0]
    k = gain.shape[0]
    return pl.pallas_call(
        kernel_fn,
        grid=(m // tm, grid_n),
        in_specs=[
            pl.BlockSpec((tm, k), lambda i, j: (i, col_block)),
            pl.BlockSpec((1, k), lambda i, j: (0, 0)),
            w_spec,
        ] + extra_specs,
        out_specs=out_spec,
        out_shape=out_shape,
        scratch_shapes=[pltpu.VMEM((tm, k), MXU_DTYPE)],
        compiler_params=_params("parallel", "arbitrary"),
        name=name,
    )(lat, gain.reshape(1, k), w, *extra)


def _q_up(lat, gain, w, tables, seq, scale):
    m = lat.shape[0]
    k, n = w.shape
    tm = _tile(seq, TM_PROJ)
    tn = _tile(n, max(TN_PROJ, NOPE_B + LANES_V7X))
    per = seq // tm
    table_spec = pl.BlockSpec((tm, LANES_V7X), lambda i, j: (i % per, 0))
    return _latent_up(
        functools.partial(_q_up_kernel, scale=scale), lat, 0, gain, w, list(tables), [table_spec, table_spec],
        pl.BlockSpec((tm, tn), lambda i, j: (i, j)), jax.ShapeDtypeStruct((m, n), MXU_DTYPE),
        tm, n // tn, pl.BlockSpec((k, tn), lambda i, j: (0, j)), "q_up")


def _k_up(lat, gain, w, kr, seq):
    m = lat.shape[0]
    k, n = w.shape
    tm = _tile(seq, TM_PROJ)
    tn = _tile(n, TN_PROJ)
    return _latent_up(
        _k_up_kernel, lat, 1, gain, w, [kr], [pl.BlockSpec((tm, LANES_V7X), lambda i, j: (i, 0))],
        pl.BlockSpec((tm, 2 * tn), lambda i, j: (i, j)), jax.ShapeDtypeStruct((m, 2 * n), MXU_DTYPE),
        tm, n // tn, pl.BlockSpec((k, tn), lambda i, j: (0, j)), "k_up")


def _v_up_t(lat, gain, wt, chunk):
    m = lat.shape[0]
    n, k = wt.shape
    tn = _tile(n, TN_PROJ)
    heads_per = tn // V_B
    return _latent_up(
        _v_up_t_kernel, lat, 1, gain, wt, [], [],
        pl.BlockSpec((1, heads_per, V_B, chunk), lambda i, j: (i, j, 0, 0)),
        jax.ShapeDtypeStruct((m // chunk, n // V_B, V_B, chunk), MXU_DTYPE),
        chunk, n // tn, pl.BlockSpec((tn, k), lambda i, j: (j, 0)), "v_up_t")


def _window_kernel(q_ref, kp_ref, ko_ref, kn_ref, vp_ref, vo_ref, vn_ref, sink_ref, o_ref, *, seq, slopes):
    n = pl.program_id(1)
    w = WINDOW
    group = N_Q_A // N_KV_A
    scale = HD_A ** -0.5
    q = q_ref[...]
    k3 = jnp.concatenate([kp_ref[...], ko_ref[...], kn_ref[...]], axis=0)
    v3 = jnp.concatenate([vp_ref[...], vo_ref[...], vn_ref[...]], axis=0)
    lane = lax.broadcasted_iota(jnp.int32, k3.shape, 1)
    lo = lane < HD_A
    zero = jnp.zeros_like(k3)
    kbd = jnp.concatenate([jnp.where(lo, k3, zero), jnp.where(lo, zero, k3)], axis=0)
    vbd = jnp.concatenate([jnp.where(lo, v3, zero), jnp.where(lo, zero, v3)], axis=0)

    row = lax.broadcasted_iota(jnp.int32, (w, 6 * w), 0)
    col = lax.broadcasted_iota(jnp.int32, (w, 6 * w), 1)
    first = col < 3 * w
    krel = jnp.where(first, col, col - 3 * w) - w
    dist = jnp.abs(row - krel)
    kabs = n * w + krel
    valid = (dist <= w) & (kabs >= 0) & (kabs < seq)
    distf = dist.astype(F32)
    out_lo = lax.broadcasted_iota(jnp.int32, (w, 2 * HD_A), 1) < HD_A

    outs = []
    for g in range(group):
        qg = q[:, g * 2 * HD_A:(g + 1) * 2 * HD_A]
        s = lax.dot_general(qg, kbd, (((1,), (1,)), ((), ())), preferred_element_type=F32) * scale
        slope = jnp.where(first, slopes[g], slopes[group + g])
        s = jnp.where(valid, s - slope * distf, -jnp.inf)
        sink0 = sink_ref[0:1, g:g + 1]
        sink1 = sink_ref[0:1, group + g:group + g + 1]
        m0 = jnp.maximum(jnp.max(s[:, :3 * w], axis=1, keepdims=True), sink0)
        m1 = jnp.maximum(jnp.max(s[:, 3 * w:], axis=1, keepdims=True), sink1)
        p = jnp.exp(s - jnp.where(first, m0, m1))
        d0 = jnp.sum(p[:, :3 * w], axis=1, keepdims=True) + jnp.exp(sink0 - m0)
        d1 = jnp.sum(p[:, 3 * w:], axis=1, keepdims=True) + jnp.exp(sink1 - m1)
        o = jnp.dot(p.astype(MXU_DTYPE), vbd, preferred_element_type=F32)
        outs.append(o * jnp.where(out_lo, 1.0 / d0, 1.0 / d1))
    o_ref[...] = jnp.concatenate(outs, axis=1).astype(o_ref.dtype)


def _window_attention(qkv, sinks, batch, seq):
    t = qkv.shape[0]
    w = WINDOW
    nb = seq // w
    qw = N_Q_A * HD_A
    kvw = N_KV_A * HD_A
    assert N_KV_A == 2 and kvw == LANES_V7X
    kblk = qw // kvw
    vblk = kblk + 1
    slopes = tuple(2.0 ** (-8.0 * (i + 1) / N_Q_A) for i in range(N_Q_A))

    def kv_spec(col, off):
        return pl.BlockSpec((w, kvw), lambda b, n: (b * nb + jnp.clip(n + off, 0, nb - 1), col))

    return pl.pallas_call(
        functools.partial(_window_kernel, seq=seq, slopes=slopes),
        grid=(batch, nb),
        in_specs=[
            pl.BlockSpec((w, qw), lambda b, n: (b * nb + n, 0)),
            kv_spec(kblk, -1), kv_spec(kblk, 0), kv_spec(kblk, 1),
            kv_spec(vblk, -1), kv_spec(vblk, 0), kv_spec(vblk, 1),
            pl.BlockSpec((1, N_Q_A), lambda b, n: (0, 0)),
        ],
        out_specs=pl.BlockSpec((w, qw), lambda b, n: (b * nb + n, 0)),
        out_shape=jax.ShapeDtypeStruct((t, qw), MXU_DTYPE),
        compiler_params=_params("parallel", "parallel"),
        name="window_attention",
    )(qkv, qkv, qkv, qkv, qkv, qkv, qkv, sinks.reshape(1, N_Q_A))


def _mla_kernel(q_ref, k_ref, vt_ref, o_ref, s0_ref, s1_ref, m_ref, l_ref, acc_ref, *, nchunk):
    tkc = s0_ref.shape[0]
    q = q_ref[...]
    m_ref[...] = jnp.full(m_ref.shape, -jnp.inf, F32)
    l_ref[...] = jnp.zeros(l_ref.shape, F32)
    acc_ref[...] = jnp.zeros(acc_ref.shape, F32)

    def scores(c, s_ref):
        start = pl.multiple_of(c * tkc, tkc)
        s_ref[...] = lax.dot_general(k_ref[pl.ds(start, tkc), :], q, (((1,), (1,)), ((), ())),
                                     preferred_element_type=F32)

    def accumulate(c, s_ref):
        s = s_ref[...]
        m_old = m_ref[...]
        m_new = jnp.maximum(m_old, jnp.max(s, axis=0, keepdims=True))
        alpha = jnp.exp2(m_old - m_new)
        p = jnp.exp2(s - m_new)
        l_ref[...] = alpha * l_ref[...] + jnp.sum(p, axis=0, keepdims=True)
        acc_ref[...] = alpha * acc_ref[...] + jnp.dot(vt_ref[c], p.astype(MXU_DTYPE), preferred_element_type=F32)
        m_ref[...] = m_new

    scores(0, s0_ref)

    def body(j, carry):
        c = 2 * j
        scores(c + 1, s1_ref)
        accumulate(c, s0_ref)
        scores(c + 2, s0_ref)
        accumulate(c + 1, s1_ref)
        return carry

    lax.fori_loop(0, nchunk // 2 - 1, body, 0)
    scores(nchunk - 1, s1_ref)
    accumulate(nchunk - 2, s0_ref)
    accumulate(nchunk - 1, s1_ref)
    o_ref[...] = (acc_ref[...] * (1.0 / l_ref[...])).T.astype(o_ref.dtype)


def _mla_attention(q, kcat, vt, batch, seq):
    t = q.shape[0]
    tkc = vt.shape[-1]
    nchunk = seq // tkc
    assert nchunk % 2 == 0
    tq = _tile(seq, TQ_MLA)
    nq = seq // tq
    qh = NOPE_B + LANES_V7X
    assert NOPE_B == LANES_V7X and V_B == LANES_V7X
    vt5 = vt.reshape(batch, nchunk, N_H_B, V_B, tkc)
    return pl.pallas_call(
        functools.partial(_mla_kernel, nchunk=nchunk),
        grid=(batch, N_H_B, nq),
        in_specs=[
            pl.BlockSpec((tq, qh), lambda b, h, qi: (b * nq + qi, h)),
            pl.BlockSpec((seq, qh), lambda b, h, qi: (b, h)),
            pl.BlockSpec((None, nchunk, None, V_B, tkc), lambda b, h, qi: (b, 0, h, 0, 0)),
        ],
        out_specs=pl.BlockSpec((tq, V_B), lambda b, h, qi: (b * nq + qi, h)),
        out_shape=jax.ShapeDtypeStruct((t, N_H_B * V_B), MXU_DTYPE),
        scratch_shapes=[
            pltpu.VMEM((tkc, tq), F32),
            pltpu.VMEM((tkc, tq), F32),
            pltpu.VMEM((1, tq), F32),
            pltpu.VMEM((1, tq), F32),
            pltpu.VMEM((V_B, tq), F32),
        ],
        compiler_params=_params("parallel", "parallel", "parallel"),
        name="mla_attention",
    )(q, kcat, vt5)


def _merge_kernel(oa_ref, ob_ref, wa_ref, wb_ref, ga_ref, gb_ref, o_ref):
    a = jnp.dot(oa_ref[...], wa_ref[...], preferred_element_type=F32)
    b = jnp.dot(ob_ref[...], wb_ref[...], preferred_element_type=F32)
    o_ref[...] = (_sigmoid(ga_ref[...]) * a + _sigmoid(gb_ref[...]) * b).astype(o_ref.dtype)


def _merge(oa, ob, wa, wb, gates):
    t = oa.shape[0]
    d = wa.shape[1]
    tm = _tile(t, TM_PROJ)
    tn = _tile(d, TN_PROJ)
    nj = d // tn
    return pl.pallas_call(
        _merge_kernel,
        grid=(t // tm, nj),
        in_specs=[
            pl.BlockSpec((tm, oa.shape[1]), lambda i, j: (i, 0)),
            pl.BlockSpec((tm, ob.shape[1]), lambda i, j: (i, 0)),
            pl.BlockSpec((wa.shape[0], tn), lambda i, j: (0, j)),
            pl.BlockSpec((wb.shape[0], tn), lambda i, j: (0, j)),
            pl.BlockSpec((tm, tn), lambda i, j: (i, j)),
            pl.BlockSpec((tm, tn), lambda i, j: (i, nj + j)),
        ],
        out_specs=pl.BlockSpec((tm, tn), lambda i, j: (i, j)),
        out_shape=jax.ShapeDtypeStruct((t, d), MXU_DTYPE),
        compiler_params=_params("parallel", "parallel"),
        name="merge",
    )(oa, ob, wa, wb, gates, gates)


def _layer_norm_rows(z, g, b):
    mu = jnp.mean(z, axis=-1, keepdims=True)
    zc = z - mu
    var = jnp.mean(zc * zc, axis=-1, keepdims=True)
    return (zc * lax.rsqrt(var + LN_EPS)) * g + b


def _ln1_kernel(mg_ref, wo_ref, x_ref, mod_ref, lng_ref, lnb_ref, wr_ref, br_ref,
                x1_ref, h2_ref, ri_ref, rg_ref, *, alpha):
    y = jnp.dot(mg_ref[...], wo_ref[...], preferred_element_type=F32)
    g1 = mod_ref[0, 2:3, :]
    sh2 = mod_ref[0, 3:4, :]
    sc2 = mod_ref[0, 4:5, :]
    x1 = _layer_norm_rows(alpha * x_ref[...] + g1 * y, lng_ref[...], lnb_ref[...])
    x1_ref[...] = x1
    h2 = x1 * (1.0 + sc2) + sh2
    h2_ref[...] = h2
    logits = jnp.dot(h2, wr_ref[...], preferred_element_type=F32, precision=lax.Precision.HIGHEST) + br_ref[...]
    ne = logits.shape[1]
    eio = lax.broadcasted_iota(jnp.int32, logits.shape, 1).astype(F32)
    vals, idxs = [], []
    for _ in range(TOP_K):
        mx = jnp.max(logits, axis=1, keepdims=True)
        ix = jnp.min(jnp.where(logits == mx, eio, float(ne)), axis=1, keepdims=True)
        vals.append(mx)
        idxs.append(ix)
        logits = jnp.where(eio == ix, -jnp.inf, logits)
    es = [jnp.exp(v - vals[0]) for v in vals]
    tot = es[0]
    for e in es[1:]:
        tot = tot + e
    lane = lax.broadcasted_iota(jnp.int32, ri_ref.shape, 1)
    ri = jnp.zeros(ri_ref.shape, F32)
    rg = jnp.zeros(rg_ref.shape, F32)
    for k in range(TOP_K):
        ri = jnp.where(lane == k, idxs[k], ri)
        rg = jnp.where(lane == k, es[k] / tot, rg)
    ri_ref[...] = ri.astype(jnp.int32)
    rg_ref[...] = rg


def _oproj_ln1_router(merged, w_o, x2d, mod3, ln_g, ln_b, w_router, b_router, seq, alpha):
    t, d = x2d.shape
    ne = w_router.shape[1]
    tm = _tile(seq, TM_LN)
    per = seq // tm
    row = lambda i: (i, 0)
    const = lambda i: (0, 0)
    return pl.pallas_call(
        functools.partial(_ln1_kernel, alpha=alpha),
        grid=(t // tm,),
        in_specs=[
            pl.BlockSpec((tm, d), row),
            pl.BlockSpec((d, d), const),
            pl.BlockSpec((tm, d), row),
            pl.BlockSpec((1, N_MOD, d), lambda i: (i // per, 0, 0)),
            pl.BlockSpec((1, d), const),
            pl.BlockSpec((1, d), const),
            pl.BlockSpec((d, ne), const),
            pl.BlockSpec((1, ne), const),
        ],
        out_specs=[
            pl.BlockSpec((tm, d), row),
            pl.BlockSpec((tm, d), row),
            pl.BlockSpec((tm, LANES_V7X), row),
            pl.BlockSpec((tm, LANES_V7X), row),
        ],
        out_shape=[
            jax.ShapeDtypeStruct((t, d), F32),
            jax.ShapeDtypeStruct((t, d), F32),
            jax.ShapeDtypeStruct((t, LANES_V7X), jnp.int32),
            jax.ShapeDtypeStruct((t, LANES_V7X), F32),
        ],
        compiler_params=_params("parallel"),
        name="oproj_ln1_router",
    )(merged, w_o, x2d, mod3, ln_g.reshape(1, d), ln_b.reshape(1, d), w_router, b_router.reshape(1, ne))


def _moe_kernel(be_ref, used_ref, tok_ref, tok_next_ref, g_ref, h_hbm, wg_ref, wu_ref, bg_ref, bu_ref, wd_ref, bd_ref,
                o_ref, xg_ref, xb_ref, acc_ref, sem):
    del be_ref
    i = pl.program_id(0)
    f = pl.program_id(1)
    nblk = pl.num_programs(0)
    bm = xg_ref.shape[1]
    used = used_ref[i] > 0
    slot = i % 2

    def start_gather(idx_ref, dst_slot):
        def issue(r, carry):
            pltpu.make_async_copy(h_hbm.at[pl.ds(idx_ref[0, 0, r], 1), :],
                                  xg_ref.at[dst_slot, pl.ds(r, 1), :], sem.at[dst_slot]).start()
            return carry

        lax.fori_loop(0, bm, issue, 0, unroll=8)

    @pl.when(jnp.logical_and(f == 0, jnp.logical_and(i == 0, used)))
    def _():
        start_gather(tok_ref, 0)

    @pl.when(jnp.logical_and(f == 0, used))
    def _():
        pltpu.make_async_copy(h_hbm.at[pl.ds(0, bm), :], xg_ref.at[slot], sem.at[slot]).wait()
        xb_ref[...] = xg_ref[slot].astype(xb_ref.dtype)
        acc_ref[...] = jnp.zeros(acc_ref.shape, F32)

    nxt = jnp.minimum(i + 1, nblk - 1)

    @pl.when(jnp.logical_and(f == 0, jnp.logical_and(i + 1 < nblk, used_ref[nxt] > 0)))
    def _():
        start_gather(tok_next_ref, 1 - slot)

    @pl.when(used)
    def _():
        x = xb_ref[...]
        gate = jnp.dot(x, wg_ref[0], preferred_element_type=F32) + bg_ref[0]
        up = jnp.dot(x, wu_ref[0], preferred_element_type=F32) + bu_ref[0]
        gate = jnp.minimum(gate, SWIGLU_LIMIT)
        up = jnp.clip(up, -SWIGLU_LIMIT, SWIGLU_LIMIT)
        act = (up + 1.0) * gate * _sigmoid(SWIGLU_ALPHA * gate)
        acc_ref[...] += jnp.dot(act.astype(MXU_DTYPE), wd_ref[0], preferred_element_type=F32)

    last = f == pl.num_programs(1) - 1

    @pl.when(jnp.logical_and(last, used))
    def _():
        o_ref[...] = (acc_ref[...] + bd_ref[0]) * g_ref[...]

    @pl.when(jnp.logical_and(last, jnp.logical_not(used)))
    def _():
        o_ref[...] = jnp.zeros(o_ref.shape, F32)


def _moe(h2, blk_e, blk_used, slot_tok, slot_g, w_gu, b_gu, w_down, b_down, bm):
    t, d = h2.shape
    ne, _, two_ff = w_gu.shape
    dff = two_ff // 2
    tf = _tile(dff, TF_MOE)
    nf = dff // tf
    p = slot_tok.shape[0]
    nblk = p // bm

    def ftile(i, f, bu):
        return jnp.where(bu[i] > 0, f, nf - 1)

    grid_spec = pltpu.PrefetchScalarGridSpec(
        num_scalar_prefetch=2,
        grid=(nblk, nf),
        in_specs=[
            pl.BlockSpec((1, 1, bm), lambda i, f, be, bu: (i, 0, 0), memory_space=pltpu.SMEM),
            pl.BlockSpec((1, 1, bm), lambda i, f, be, bu: (jnp.minimum(i + 1, nblk - 1), 0, 0),
                         memory_space=pltpu.SMEM),
            pl.BlockSpec((bm, 1), lambda i, f, be, bu: (i, 0)),
            pl.BlockSpec(memory_space=pl.ANY),
            pl.BlockSpec((1, d, tf), lambda i, f, be, bu: (be[i], 0, ftile(i, f, bu))),
            pl.BlockSpec((1, d, tf), lambda i, f, be, bu: (be[i], 0, nf + ftile(i, f, bu))),
            pl.BlockSpec((1, 1, tf), lambda i, f, be, bu: (be[i], 0, ftile(i, f, bu))),
            pl.BlockSpec((1, 1, tf), lambda i, f, be, bu: (be[i], 0, nf + ftile(i, f, bu))),
            pl.BlockSpec((1, tf, d), lambda i, f, be, bu: (be[i], ftile(i, f, bu), 0)),
            pl.BlockSpec((1, 1, d), lambda i, f, be, bu: (be[i], 0, 0)),
        ],
        out_specs=pl.BlockSpec((bm, d), lambda i, f, be, bu: (i, 0)),
        scratch_shapes=[
            pltpu.VMEM((2, bm, d), F32),
            pltpu.VMEM((bm, d), MXU_DTYPE),
            pltpu.VMEM((bm, d), F32),
            pltpu.SemaphoreType.DMA((2,)),
        ],
    )
    slot_tok3 = slot_tok.reshape(nblk, 1, bm)
    return pl.pallas_call(
        _moe_kernel,
        grid_spec=grid_spec,
        out_shape=jax.ShapeDtypeStruct((p, d), F32),
        compiler_params=_params("arbitrary", "arbitrary"),
        name="moe_experts",
    )(blk_e, blk_used, slot_tok3, slot_tok3, slot_g.reshape(p, 1), h2,
      w_gu, w_gu, b_gu.reshape(ne, 1, two_ff), b_gu.reshape(ne, 1, two_ff), w_down, b_down.reshape(ne, 1, d))


def _final_kernel(dst_ref, dst_next_ref, x1_ref, mod_ref, lng_ref, lnb_ref, yb_hbm, o_ref, buf_ref, sem, *, alpha):
    i = pl.program_id(0)
    nt = pl.num_programs(0)
    tm = x1_ref.shape[0]
    nrow = buf_ref.shape[1]
    slot = i % 2

    def start_gather(idx_ref, dst_slot):
        def issue(r, carry):
            pltpu.make_async_copy(yb_hbm.at[pl.ds(idx_ref[0, 0, r], 1), :],
                                  buf_ref.at[dst_slot, pl.ds(r, 1), :], sem.at[dst_slot]).start()
            return carry

        lax.fori_loop(0, nrow, issue, 0, unroll=8)

    @pl.when(i == 0)
    def _():
        start_gather(dst_ref, 0)

    @pl.when(i + 1 < nt)
    def _():
        start_gather(dst_next_ref, 1 - slot)

    pltpu.make_async_copy(yb_hbm.at[pl.ds(0, nrow), :], buf_ref.at[slot], sem.at[slot]).wait()
    y = buf_ref[slot, 0:tm, :]
    for k in range(1, TOP_K):
        y = y + buf_ref[slot, k * tm:(k + 1) * tm, :]
    g2 = mod_ref[0, 5:6, :]
    o_ref[...] = _layer_norm_rows(alpha * x1_ref[...] + g2 * y, lng_ref[...], lnb_ref[...])


def _combine_ln2(dst, x1, mod3, ln_g, ln_b, yb, seq, alpha):
    t, d = x1.shape
    tm = _tile(seq, TM_FINAL)
    per = seq // tm
    nt = t // tm
    dst_tiles = dst.reshape(nt, tm, TOP_K).transpose(0, 2, 1).reshape(nt, 1, TOP_K * tm)
    return pl.pallas_call(
        functools.partial(_final_kernel, alpha=alpha),
        grid=(nt,),
        in_specs=[
            pl.BlockSpec((1, 1, TOP_K * tm), lambda i: (i, 0, 0), memory_space=pltpu.SMEM),
            pl.BlockSpec((1, 1, TOP_K * tm), lambda i: (jnp.minimum(i + 1, nt - 1), 0, 0), memory_space=pltpu.SMEM),
            pl.BlockSpec((tm, d), lambda i: (i, 0)),
            pl.BlockSpec((1, N_MOD, d), lambda i: (i // per, 0, 0)),
            pl.BlockSpec((1, d), lambda i: (0, 0)),
            pl.BlockSpec((1, d), lambda i: (0, 0)),
            pl.BlockSpec(memory_space=pl.ANY),
        ],
        out_specs=pl.BlockSpec((tm, d), lambda i: (i, 0)),
        out_shape=jax.ShapeDtypeStruct((t, d), F32),
        scratch_shapes=[
            pltpu.VMEM((2, TOP_K * tm, d), F32),
            pltpu.SemaphoreType.DMA((2,)),
        ],
        compiler_params=_params("arbitrary"),
        name="combine_ln2",
    )(dst_tiles, dst_tiles, x1, mod3, ln_g.reshape(1, d), ln_b.reshape(1, d), yb)


def _route_slots(top_i, gates, n_experts, bm):
    t = top_i.shape[0]
    na = t * TOP_K
    flat_e = top_i.reshape(-1)
    order = jnp.argsort(flat_e).astype(jnp.int32)
    rank = jnp.argsort(order).astype(jnp.int32)
    experts = jnp.arange(n_experts, dtype=jnp.int32)
    counts = jnp.sum((flat_e[:, None] == experts[None, :]).astype(jnp.int32), axis=0)
    starts = jnp.cumsum(counts) - counts
    padded = ((counts + bm - 1) // bm) * bm
    pends = jnp.cumsum(padded)
    pstarts = pends - padded
    dest = (pstarts[flat_e] + rank - starts[flat_e]).reshape(t, TOP_K)
    p = na + n_experts * bm
    nblk = p // bm
    slot = jnp.arange(p, dtype=jnp.int32)
    slot_e = jnp.minimum(jnp.sum((slot[:, None] >= pends[None, :]).astype(jnp.int32), axis=1), n_experts - 1)
    within = slot - pstarts[slot_e]
    real = within < counts[slot_e]
    assign = order[jnp.clip(starts[slot_e] + within, 0, na - 1)]
    slot_tok = jnp.where(real, assign // TOP_K, 0)
    slot_g = jnp.where(real, gates.reshape(-1)[assign], 0.0)
    blk_start = jnp.arange(nblk, dtype=jnp.int32) * bm
    blk_e = jnp.minimum(jnp.sum((blk_start[:, None] >= pends[None, :]).astype(jnp.int32), axis=1), n_experts - 1)
    blk_used = (blk_start < pends[-1]).astype(jnp.int32)
    return slot_tok, slot_g, dest, blk_e, blk_used


def _pad_rope_cols(w):
    half = ROPE_B // 2
    z = jnp.zeros(w.shape[:-1] + (LANES_V7X // 2 - half,), w.dtype)
    return jnp.concatenate([w[..., :half], z, w[..., half:], z], axis=-1)


def _rope_tables(seq):
    half = ROPE_B // 2
    pos = jnp.arange(seq, dtype=F32)
    freqs = ROPE_THETA ** (-jnp.arange(half, dtype=F32) / half)
    ang = pos[:, None] * freqs[None, :]
    cos = jnp.cos(ang)
    sin = jnp.sin(ang)
    z = jnp.zeros((seq, LANES_V7X // 2 - half), F32)
    return (jnp.concatenate([cos, z, cos, z], axis=1), jnp.concatenate([-sin, z, sin, z], axis=1))


def _prepare_layer(w_in, w_uq, w_ukv, w_a_out, w_b_out, w_o, w_gu, w_down):
    d = w_in.shape[0]
    qa_w = N_Q_A * HD_A
    ka_w = N_KV_A * HD_A
    q_lora = w_uq.shape[0]
    kv_lora = w_ukv.shape[0]
    assert q_lora == kv_lora
    group = N_Q_A // N_KV_A
    offs = [0]
    for wdt in (qa_w, ka_w, ka_w, q_lora, kv_lora, ROPE_B, d, d):
        offs.append(offs[-1] + wdt)
    assert offs[-1] == w_in.shape[1]
    part = lambda i: w_in[:, offs[i]:offs[i + 1]]
    w_qa = part(0).reshape(d, N_KV_A, group, HD_A).transpose(0, 2, 1, 3).reshape(d, qa_w)
    cast = lambda w: w.astype(MXU_DTYPE)
    w_uq3 = w_uq.reshape(q_lora, N_H_B, NOPE_B + ROPE_B)
    w_uq_p = jnp.concatenate([w_uq3[..., :NOPE_B], _pad_rope_cols(w_uq3[..., NOPE_B:])], axis=-1)
    w_ukv3 = w_ukv.reshape(kv_lora, N_H_B, NOPE_B + V_B)
    return dict(
        w_qkv=cast(jnp.concatenate([w_qa, part(1), part(2)], axis=1)),
        w_lat=cast(jnp.concatenate([part(3), part(4)], axis=1)),
        w_kr=cast(_pad_rope_cols(part(5))),
        w_gates=cast(jnp.concatenate([part(6), part(7)], axis=1)),
        w_uq=cast(w_uq_p.reshape(q_lora, N_H_B * (NOPE_B + LANES_V7X))),
        w_uk=cast(w_ukv3[..., :NOPE_B].reshape(kv_lora, N_H_B * NOPE_B)),
        w_uv_t=cast(w_ukv3[..., NOPE_B:].reshape(kv_lora, N_H_B * V_B).T),
        w_a_out=cast(w_a_out.reshape(N_KV_A, group, HD_A, d).transpose(1, 0, 2, 3).reshape(qa_w, d)),
        w_b_out=cast(w_b_out),
        w_o=cast(w_o),
        w_gu=cast(w_gu),
        w_down=cast(w_down),
    )


def _encoder_layer(x, mod, wts, sinks_a, q_norm_g, kv_norm_g, ln1_g, ln1_b, w_router, b_router,
                   b_gu, b_down, ln2_g, ln2_b, alpha):
    b, s, d = x.shape
    t = b * s
    x2d = x.reshape(t, d)
    mod3 = mod.reshape(b, N_MOD, d)
    tables = _rope_tables(s)

    h = _modulate(x2d, mod3, s)
    qkv = _matmul(h, wts['w_qkv'], MXU_DTYPE, "proj_qkv_a")
    lat = _matmul(h, wts['w_lat'], F32, "proj_latents")
    gates = _matmul(h, wts['w_gates'], F32, "proj_gates")
    kr = _matmul_rope(h, wts['w_kr'], tables[0], tables[1], s, "proj_k_rope")

    oa = _window_attention(qkv, sinks_a, b, s)

    qscale = (NOPE_B + ROPE_B) ** -0.5 * LOG2_E
    q = _q_up(lat, q_norm_g, wts['w_uq'], tables, s, qscale)
    kcat = _k_up(lat, kv_norm_g, wts['w_uk'], kr, s)
    vt = _v_up_t(lat, kv_norm_g, wts['w_uv_t'], _tile(s, TKC_MLA))
    ob = _mla_attention(q, kcat, vt, b, s)

    merged = _merge(oa, ob, wts['w_a_out'], wts['w_b_out'], gates)
    x1, h2, ridx, rgate = _oproj_ln1_router(merged, wts['w_o'], x2d, mod3, ln1_g, ln1_b, w_router, b_router, s, alpha)

    n_experts = w_router.shape[1]
    bm = _tile(t, BM_MOE)
    slot_tok, slot_g, dest, blk_e, blk_used = _route_slots(ridx[:, :TOP_K], rgate[:, :TOP_K], n_experts, bm)
    yb = _moe(h2, blk_e, blk_used, slot_tok, slot_g, wts['w_gu'], b_gu, wts['w_down'], b_down, bm)
    out = _combine_ln2(dest, x1, mod3, ln2_g, ln2_b, yb, s, alpha)
    return out.reshape(b, s, d)


def kernel(x_prompt, x_sample, c_prompt, c_sample, w_ada, b_ada, w_in, sinks_a, q_norm_g, kv_norm_g, w_uq, w_ukv, w_a_out, w_b_out, w_o, ln1_g, ln1_b, w_router, b_router, w_gu, b_gu, w_down, b_down, ln2_g, ln2_b):
    depth = w_ada.shape[0]
    alpha = (2.0 * depth) ** 0.25
    nbp = c_prompt.shape[0]
    nbs = c_sample.shape[0]
    sub = 8
    pad = (-(nbp + nbs)) % sub
    y_prompt, y_sample = x_prompt, x_sample
    for l in range(depth):
        c_all = jnp.concatenate([c_prompt, c_sample, jnp.zeros((pad, c_prompt.shape[1]), F32)], axis=0)
        mod = _ada(c_all, w_ada[l], b_ada[l])
        wts = _prepare_layer(w_in[l], w_uq[l], w_ukv[l], w_a_out[l], w_b_out[l], w_o[l], w_gu[l], w_down[l])
        rest = (wts, sinks_a[l], q_norm_g[l], kv_norm_g[l], ln1_g[l], ln1_b[l], w_router[l], b_router[l],
                b_gu[l], b_down[l], ln2_g[l], ln2_b[l], alpha)
        y_prompt = _encoder_layer(y_prompt, mod[:nbp], *rest)
        y_sample = _encoder_layer(y_sample, mod[nbp:nbp + nbs], *rest)
    return (y_prompt, y_sample)
```

```python
import functools

import jax
import jax.numpy as jnp
from jax import lax
from jax.experimental import pallas as pl
from jax.experimental.pallas import tpu as pltpu

N_Q_A = 16
N_KV_A = 2
HD_A = 64
WINDOW = 128
N_H_B = 16
NOPE_B = 128
ROPE_B = 64
V_B = 128
ROPE_THETA = 10000.0
TOP_K = 4
SWIGLU_LIMIT = 7.0
SWIGLU_ALPHA = 1.702
LN_EPS = 1e-5
RMS_EPS = 1e-6
N_MOD = 6
LOG2_E = 1.4426950408889634

LANES_V7X = 128
SUBLANES_V7X = 8
VMEM_LIMIT_BYTES_V7X = 56 * 1024 * 1024

MXU_DTYPE = jnp.bfloat16
F32 = jnp.float32

TM_PROJ = 1024
TN_PROJ = 512
TM_LN = 256
TQ_MLA = 1024
TKC_MLA = 512
BM_MOE = 512
TF_MOE = 512
TM_FINAL = 128
TN_ADA = 1024


def _tile(n, pref):
    if n <= pref:
        return n
    align = LANES_V7X if pref >= LANES_V7X else SUBLANES_V7X
    t = pref - pref % align
    while t > 0 and n % t:
        t -= align
    assert t > 0, (n, pref)
    return t


def _params(*sem):
    return pltpu.CompilerParams(dimension_semantics=sem, vmem_limit_bytes=VMEM_LIMIT_BYTES_V7X)


def _sigmoid(x):
    return 1.0 / (1.0 + jnp.exp(-x))


def _rope_pairs(x, cos, sin):
    return x * cos + pltpu.roll(x, LANES_V7X // 2, axis=1) * sin


def _ada_kernel(c_ref, w_ref, b_ref, o_ref):
    c = c_ref[...]
    a = (c * _sigmoid(c)).astype(MXU_DTYPE)
    o_ref[...] = jnp.dot(a, w_ref[...].astype(MXU_DTYPE), preferred_element_type=F32) + b_ref[...]


def _ada(c, w_ada, b_ada):
    bp, d = c.shape
    n = w_ada.shape[1]
    tn = _tile(n, TN_ADA)
    return pl.pallas_call(
        _ada_kernel,
        grid=(n // tn,),
        in_specs=[
            pl.BlockSpec((bp, d), lambda j: (0, 0)),
            pl.BlockSpec((d, tn), lambda j: (0, j)),
            pl.BlockSpec((1, tn), lambda j: (0, j)),
        ],
        out_specs=pl.BlockSpec((bp, tn), lambda j: (0, j)),
        out_shape=jax.ShapeDtypeStruct((bp, n), F32),
        compiler_params=_params("parallel"),
        name="ada",
    )(c, w_ada, b_ada.reshape(1, n))


def _mod_kernel(x_ref, mod_ref, o_ref):
    sh = mod_ref[0, 0:1, :]
    sc = mod_ref[0, 1:2, :]
    o_ref[...] = (x_ref[...] * (1.0 + sc) + sh).astype(o_ref.dtype)


def _modulate(x2d, mod3, seq):
    t, d = x2d.shape
    tm = _tile(seq, TM_PROJ)
    per = seq // tm
    return pl.pallas_call(
        _mod_kernel,
        grid=(t // tm,),
        in_specs=[
            pl.BlockSpec((tm, d), lambda i: (i, 0)),
            pl.BlockSpec((1, N_MOD, d), lambda i: (i // per, 0, 0)),
        ],
        out_specs=pl.BlockSpec((tm, d), lambda i: (i, 0)),
        out_shape=jax.ShapeDtypeStruct((t, d), MXU_DTYPE),
        compiler_params=_params("parallel"),
        name="modulate1",
    )(x2d, mod3)


def _mm_kernel(x_ref, w_ref, o_ref):
    o_ref[...] = jnp.dot(x_ref[...], w_ref[...], preferred_element_type=F32).astype(o_ref.dtype)


def _matmul(x, w, out_dtype, name):
    m, k = x.shape
    n = w.shape[1]
    tm = _tile(m, TM_PROJ)
    tn = _tile(n, TN_PROJ)
    return pl.pallas_call(
        _mm_kernel,
        grid=(m // tm, n // tn),
        in_specs=[
            pl.BlockSpec((tm, k), lambda i, j: (i, 0)),
            pl.BlockSpec((k, tn), lambda i, j: (0, j)),
        ],
        out_specs=pl.BlockSpec((tm, tn), lambda i, j: (i, j)),
        out_shape=jax.ShapeDtypeStruct((m, n), out_dtype),
        compiler_params=_params("parallel", "parallel"),
        name=name,
    )(x, w)


def _mm_rope_kernel(x_ref, w_ref, cos_ref, sin_ref, o_ref):
    acc = jnp.dot(x_ref[...], w_ref[...], preferred_element_type=F32)
    o_ref[...] = _rope_pairs(acc, cos_ref[...], sin_ref[...]).astype(o_ref.dtype)


def _matmul_rope(x, w, cos, sin, seq, name):
    m, k = x.shape
    n = w.shape[1]
    assert n == LANES_V7X
    tm = _tile(seq, TM_PROJ)
    per = seq // tm
    return pl.pallas_call(
        _mm_rope_kernel,
        grid=(m // tm,),
        in_specs=[
            pl.BlockSpec((tm, k), lambda i: (i, 0)),
            pl.BlockSpec((k, n), lambda i: (0, 0)),
            pl.BlockSpec((tm, n), lambda i: (i % per, 0)),
            pl.BlockSpec((tm, n), lambda i: (i % per, 0)),
        ],
        out_specs=pl.BlockSpec((tm, n), lambda i: (i, 0)),
        out_shape=jax.ShapeDtypeStruct((m, n), MXU_DTYPE),
        compiler_params=_params("parallel"),
        name=name,
    )(x, w, cos, sin)


def _rms_norm_to(xn_ref, x_ref, g_ref):
    @pl.when(pl.program_id(1) == 0)
    def _():
        xf = x_ref[...]
        r = lax.rsqrt(jnp.mean(xf * xf, axis=-1, keepdims=True) + RMS_EPS)
        xn_ref[...] = ((xf * r) * g_ref[...]).astype(xn_ref.dtype)


def _q_up_kernel(x_ref, g_ref, w_ref, cos_ref, sin_ref, o_ref, xn_ref, *, scale):
    _rms_norm_to(xn_ref, x_ref, g_ref)
    acc = jnp.dot(xn_ref[...], w_ref[...], preferred_element_type=F32) * scale
    cos = cos_ref[...]
    sin = sin_ref[...]
    head_w = NOPE_B + LANES_V7X
    pieces = []
    for c in range(acc.shape[1] // head_w):
        pieces.append(acc[:, c * head_w:c * head_w + NOPE_B])
        pieces.append(_rope_pairs(acc[:, c * head_w + NOPE_B:(c + 1) * head_w], cos, sin))
    o_ref[...] = jnp.concatenate(pieces, axis=1).astype(o_ref.dtype)


def _k_up_kernel(x_ref, g_ref, w_ref, kr_ref, o_ref, xn_ref):
    _rms_norm_to(xn_ref, x_ref, g_ref)
    acc = jnp.dot(xn_ref[...], w_ref[...], preferred_element_type=F32).astype(o_ref.dtype)
    kr = kr_ref[...]
    pieces = []
    for c in range(acc.shape[1] // NOPE_B):
        pieces.append(acc[:, c * NOPE_B:(c + 1) * NOPE_B])
        pieces.append(kr)
    o_ref[...] = jnp.concatenate(pieces, axis=1)


def _v_up_t_kernel(x_ref, g_ref, wt_ref, o_ref, xn_ref):
    _rms_norm_to(xn_ref, x_ref, g_ref)
    acc = lax.dot_general(wt_ref[...], xn_ref[...], (((1,), (1,)), ((), ())), preferred_element_type=F32)
    o_ref[0] = acc.reshape(o_ref.shape[1:]).astype(o_ref.dtype)


def _latent_up(kernel_fn, lat, col_block, gain, w, extra, extra_specs, out_spec, out_shape, tm, grid_n, w_spec, name):
    m = lat.shape[0]
    k = gain.shape[0]
    return pl.pallas_call(
        kernel_fn,
        grid=(m // tm, grid_n),
        in_specs=[
            pl.BlockSpec((tm, k), lambda i, j: (i, col_block)),
            pl.BlockSpec((1, k), lambda i, j: (0, 0)),
            w_spec,
        ] + extra_specs,
        out_specs=out_spec,
        out_shape=out_shape,
        scratch_shapes=[pltpu.VMEM((tm, k), MXU_DTYPE)],
        compiler_params=_params("parallel", "arbitrary"),
        name=name,
    )(lat, gain.reshape(1, k), w, *extra)


def _q_up(lat, gain, w, tables, seq, scale):
    m = lat.shape[0]
    k, n = w.shape
    tm = _tile(seq, TM_PROJ)
    tn = _tile(n, max(TN_PROJ, NOPE_B + LANES_V7X))
    per = seq // tm
    table_spec = pl.BlockSpec((tm, LANES_V7X), lambda i, j: (i % per, 0))
    return _latent_up(
        functools.partial(_q_up_kernel, scale=scale), lat, 0, gain, w, list(tables), [table_spec, table_spec],
        pl.BlockSpec((tm, tn), lambda i, j: (i, j)), jax.ShapeDtypeStruct((m, n), MXU_DTYPE),
        tm, n // tn, pl.BlockSpec((k, tn), lambda i, j: (0, j)), "q_up")


def _k_up(lat, gain, w, kr, seq):
    m = lat.shape[0]
    k, n = w.shape
    tm = _tile(seq, TM_PROJ)
    tn = _tile(n, TN_PROJ)
    return _latent_up(
        _k_up_kernel, lat, 1, gain, w, [kr], [pl.BlockSpec((tm, LANES_V7X), lambda i, j: (i, 0))],
        pl.BlockSpec((tm, 2 * tn), lambda i, j: (i, j)), jax.ShapeDtypeStruct((m, 2 * n), MXU_DTYPE),
        tm, n // tn, pl.BlockSpec((k, tn), lambda i, j: (0, j)), "k_up")


def _v_up_t(lat, gain, wt, chunk):
    m = lat.shape[0]
    n, k = wt.shape
    tn = _tile(n, TN_PROJ)
    heads_per = tn // V_B
    return _latent_up(
        _v_up_t_kernel, lat, 1, gain, wt, [], [],
        pl.BlockSpec((1, heads_per, V_B, chunk), lambda i, j: (i, j, 0, 0)),
        jax.ShapeDtypeStruct((m // chunk, n // V_B, V_B, chunk), MXU_DTYPE),
        chunk, n // tn, pl.BlockSpec((tn, k), lambda i, j: (j, 0)), "v_up_t")


def _window_kernel(q_ref, kp_ref, ko_ref, kn_ref, vp_ref, vo_ref, vn_ref, bias_ref, sink_ref, o_ref):
    w = WINDOW
    group = N_Q_A // N_KV_A
    q = q_ref[...]
    k3 = jnp.concatenate([kp_ref[...], ko_ref[...], kn_ref[...]], axis=0)
    v3 = jnp.concatenate([vp_ref[...], vo_ref[...], vn_ref[...]], axis=0)
    lane = lax.broadcasted_iota(jnp.int32, k3.shape, 1)
    lo = lane < HD_A
    zero = jnp.zeros_like(k3)
    kbd = jnp.concatenate([jnp.where(lo, k3, zero), jnp.where(lo, zero, k3)], axis=0)
    vbd = jnp.concatenate([jnp.where(lo, v3, zero), jnp.where(lo, zero, v3)], axis=0)
    vbd_t = vbd.astype(F32).T.astype(MXU_DTYPE)
    out_lo = lax.broadcasted_iota(jnp.int32, (2 * HD_A, w), 0) < HD_A

    outs = []
    for g in range(group):
        qg = q[:, g * 2 * HD_A:(g + 1) * 2 * HD_A]
        s = lax.dot_general(kbd, qg, (((1,), (1,)), ((), ())), preferred_element_type=F32) + bias_ref[0, g]
        s0 = s[:3 * w]
        s1 = s[3 * w:]
        sink0 = sink_ref[0:1, g:g + 1] * LOG2_E
        sink1 = sink_ref[0:1, group + g:group + g + 1] * LOG2_E
        m0 = jnp.maximum(jnp.max(s0, axis=0, keepdims=True), sink0)
        m1 = jnp.maximum(jnp.max(s1, axis=0, keepdims=True), sink1)
        p0 = jnp.exp2(s0 - m0)
        p1 = jnp.exp2(s1 - m1)
        d0 = jnp.sum(p0, axis=0, keepdims=True) + jnp.exp2(sink0 - m0)
        d1 = jnp.sum(p1, axis=0, keepdims=True) + jnp.exp2(sink1 - m1)
        p = jnp.concatenate([p0, p1], axis=0).astype(MXU_DTYPE)
        o_t = jnp.dot(vbd_t, p, preferred_element_type=F32)
        outs.append(o_t * jnp.where(out_lo, 1.0 / d0, 1.0 / d1))
    o_ref[...] = jnp.concatenate(outs, axis=0).T.astype(o_ref.dtype)


def _window_bias_tables():
    w = WINDOW
    group = N_Q_A // N_KV_A
    key = jnp.arange(6 * w)
    head = key // (3 * w)
    krel = key % (3 * w) - w
    dist = jnp.abs(krel[:, None] - jnp.arange(w)[None, :])
    slopes = jnp.exp2(-8.0 * jnp.arange(1, N_Q_A + 1, dtype=F32) / N_Q_A).reshape(N_KV_A, group)
    slope = slopes[head, :].T
    alibi = -slope[:, :, None] * dist.astype(F32)[None] * LOG2_E
    tables = []
    for case in range(4):
        ok = dist <= w
        if case & 1:
            ok = ok & (krel >= 0)[:, None]
        if case & 2:
            ok = ok & (krel < w)[:, None]
        tables.append(jnp.where(ok[None], alibi, -jnp.inf))
    return jnp.stack(tables)


def _window_attention(qkv, sinks, batch, seq):
    t = qkv.shape[0]
    w = WINDOW
    nb = seq // w
    qw = N_Q_A * HD_A
    kvw = N_KV_A * HD_A
    group = N_Q_A // N_KV_A
    assert N_KV_A == 2 and kvw == LANES_V7X
    kblk = qw // kvw
    vblk = kblk + 1

    def kv_spec(col, off):
        return pl.BlockSpec((w, kvw), lambda b, n: (b * nb + jnp.clip(n + off, 0, nb - 1), col))

    def edge_case(b, n):
        return ((n == 0).astype(jnp.int32) + 2 * (n == nb - 1).astype(jnp.int32), 0, 0, 0)

    return pl.pallas_call(
        _window_kernel,
        grid=(batch, nb),
        in_specs=[
            pl.BlockSpec((w, qw), lambda b, n: (b * nb + n, 0)),
            kv_spec(kblk, -1), kv_spec(kblk, 0), kv_spec(kblk, 1),
            kv_spec(vblk, -1), kv_spec(vblk, 0), kv_spec(vblk, 1),
            pl.BlockSpec((1, group, 6 * w, w), edge_case),
            pl.BlockSpec((1, N_Q_A), lambda b, n: (0, 0)),
        ],
        out_specs=pl.BlockSpec((w, qw), lambda b, n: (b * nb + n, 0)),
        out_shape=jax.ShapeDtypeStruct((t, qw), MXU_DTYPE),
        compiler_params=_params("parallel", "parallel"),
        name="window_attention",
    )(qkv, qkv, qkv, qkv, qkv, qkv, qkv, _window_bias_tables(), sinks.reshape(1, N_Q_A))


def _mla_kernel(q_ref, k_ref, vt_ref, o_ref, s0_ref, s1_ref, m_ref, l_ref, acc_ref, *, nchunk):
    tkc = s0_ref.shape[0]
    q = q_ref[...]
    m_ref[...] = jnp.full(m_ref.shape, -jnp.inf, F32)
    l_ref[...] = jnp.zeros(l_ref.shape, F32)
    acc_ref[...] = jnp.zeros(acc_ref.shape, F32)

    def scores(c, s_ref):
        start = pl.multiple_of(c * tkc, tkc)
        s_ref[...] = lax.dot_general(k_ref[pl.ds(start, tkc), :], q, (((1,), (1,)), ((), ())),
                                     preferred_element_type=F32)

    def accumulate(c, s_ref):
        s = s_ref[...]
        m_old = m_ref[...]
        m_new = jnp.maximum(m_old, jnp.max(s, axis=0, keepdims=True))
        alpha = jnp.exp2(m_old - m_new)
        p = jnp.exp2(s - m_new)
        l_ref[...] = alpha * l_ref[...] + jnp.sum(p, axis=0, keepdims=True)
        acc_ref[...] = alpha * acc_ref[...] + jnp.dot(vt_ref[c], p.astype(MXU_DTYPE), preferred_element_type=F32)
        m_ref[...] = m_new

    scores(0, s0_ref)

    def body(j, carry):
        c = 2 * j
        scores(c + 1, s1_ref)
        accumulate(c, s0_ref)
        scores(c + 2, s0_ref)
        accumulate(c + 1, s1_ref)
        return carry

    lax.fori_loop(0, nchunk // 2 - 1, body, 0)
    scores(nchunk - 1, s1_ref)
    accumulate(nchunk - 2, s0_ref)
    accumulate(nchunk - 1, s1_ref)
    o_ref[...] = (acc_ref[...] * (1.0 / l_ref[...])).T.astype(o_ref.dtype)


def _mla_attention(q, kcat, vt, batch, seq):
    t = q.shape[0]
    tkc = vt.shape[-1]
    nchunk = seq // tkc
    assert nchunk % 2 == 0
    tq = _tile(seq, TQ_MLA)
    nq = seq // tq
    qh = NOPE_B + LANES_V7X
    assert NOPE_B == LANES_V7X and V_B == LANES_V7X
    vt5 = vt.reshape(batch, nchunk, N_H_B, V_B, tkc)
    return pl.pallas_call(
        functools.partial(_mla_kernel, nchunk=nchunk),
        grid=(batch, N_H_B, nq),
        in_specs=[
            pl.BlockSpec((tq, qh), lambda b, h, qi: (b * nq + qi, h)),
            pl.BlockSpec((seq, qh), lambda b, h, qi: (b, h)),
            pl.BlockSpec((None, nchunk, None, V_B, tkc), lambda b, h, qi: (b, 0, h, 0, 0)),
        ],
        out_specs=pl.BlockSpec((tq, V_B), lambda b, h, qi: (b * nq + qi, h)),
        out_shape=jax.ShapeDtypeStruct((t, N_H_B * V_B), MXU_DTYPE),
        scratch_shapes=[
            pltpu.VMEM((tkc, tq), F32),
            pltpu.VMEM((tkc, tq), F32),
            pltpu.VMEM((1, tq), F32),
            pltpu.VMEM((1, tq), F32),
            pltpu.VMEM((V_B, tq), F32),
        ],
        compiler_params=_params("parallel", "parallel", "parallel"),
        name="mla_attention",
    )(q, kcat, vt5)


def _merge_kernel(oa_ref, ob_ref, wa_ref, wb_ref, ga_ref, gb_ref, o_ref):
    a = jnp.dot(oa_ref[...], wa_ref[...], preferred_element_type=F32)
    b = jnp.dot(ob_ref[...], wb_ref[...], preferred_element_type=F32)
    ga = ga_ref[...].astype(F32)
    gb = gb_ref[...].astype(F32)
    o_ref[...] = (_sigmoid(ga) * a + _sigmoid(gb) * b).astype(o_ref.dtype)


def _merge(oa, ob, wa, wb, gates):
    t = oa.shape[0]
    d = wa.shape[1]
    tm = _tile(t, TM_PROJ)
    tn = _tile(d, TN_PROJ)
    nj = d // tn
    return pl.pallas_call(
        _merge_kernel,
        grid=(t // tm, nj),
        in_specs=[
            pl.BlockSpec((tm, oa.shape[1]), lambda i, j: (i, 0)),
            pl.BlockSpec((tm, ob.shape[1]), lambda i, j: (i, 0)),
            pl.BlockSpec((wa.shape[0], tn), lambda i, j: (0, j)),
            pl.BlockSpec((wb.shape[0], tn), lambda i, j: (0, j)),
            pl.BlockSpec((tm, tn), lambda i, j: (i, j)),
            pl.BlockSpec((tm, tn), lambda i, j: (i, nj + j)),
        ],
        out_specs=pl.BlockSpec((tm, tn), lambda i, j: (i, j)),
        out_shape=jax.ShapeDtypeStruct((t, d), MXU_DTYPE),
        compiler_params=_params("parallel", "parallel"),
        name="merge",
    )(oa, ob, wa, wb, gates, gates)


def _layer_norm_rows(z, g, b):
    mu = jnp.mean(z, axis=-1, keepdims=True)
    zc = z - mu
    var = jnp.mean(zc * zc, axis=-1, keepdims=True)
    return (zc * lax.rsqrt(var + LN_EPS)) * g + b


def _ln1_kernel(mg_ref, wo_ref, x_ref, mod_ref, lng_ref, lnb_ref, wr_ref, br_ref,
                x1_ref, h2_ref, ri_ref, rg_ref, *, alpha):
    y = jnp.dot(mg_ref[...], wo_ref[...], preferred_element_type=F32)
    g1 = mod_ref[0, 2:3, :]
    sh2 = mod_ref[0, 3:4, :]
    sc2 = mod_ref[0, 4:5, :]
    x1 = _layer_norm_rows(alpha * x_ref[...] + g1 * y, lng_ref[...], lnb_ref[...])
    x1_ref[...] = x1
    h2 = x1 * (1.0 + sc2) + sh2
    h2_ref[...] = h2
    h_hi = h2.astype(MXU_DTYPE)
    h_lo = (h2 - h_hi.astype(F32)).astype(MXU_DTYPE)
    r_hi = jnp.dot(h_hi, wr_ref[...], preferred_element_type=F32)
    r_lo = jnp.dot(h_lo, wr_ref[:, :LANES_V7X], preferred_element_type=F32)
    logits = r_hi[:, :LANES_V7X] + r_hi[:, LANES_V7X:] + r_lo + br_ref[...]
    ne = logits.shape[1]
    eio = lax.broadcasted_iota(jnp.int32, logits.shape, 1).astype(F32)
    vals, idxs = [], []
    for _ in range(TOP_K):
        mx = jnp.max(logits, axis=1, keepdims=True)
        ix = jnp.min(jnp.where(logits == mx, eio, float(ne)), axis=1, keepdims=True)
        vals.append(mx)
        idxs.append(ix)
        logits = jnp.where(eio == ix, -jnp.inf, logits)
    es = [jnp.exp(v - vals[0]) for v in vals]
    tot = es[0]
    for e in es[1:]:
        tot = tot + e
    lane = lax.broadcasted_iota(jnp.int32, ri_ref.shape, 1)
    ri = jnp.zeros(ri_ref.shape, F32)
    rg = jnp.zeros(rg_ref.shape, F32)
    for k in range(TOP_K):
        ri = jnp.where(lane == k, idxs[k], ri)
        rg = jnp.where(lane == k, es[k] / tot, rg)
    ri_ref[...] = ri.astype(jnp.int32)
    rg_ref[...] = rg


def _oproj_ln1_router(merged, w_o, x2d, mod3, ln_g, ln_b, w_router, b_router, seq, alpha):
    t, d = x2d.shape
    ne = w_router.shape[1]
    assert ne <= LANES_V7X
    tm = _tile(seq, TM_LN)
    per = seq // tm
    row = lambda i: (i, 0)
    const = lambda i: (0, 0)
    w_hi = w_router.astype(MXU_DTYPE)
    w_lo = (w_router - w_hi.astype(F32)).astype(MXU_DTYPE)
    lane_pad = ((0, 0), (0, LANES_V7X - ne))
    w_split = jnp.concatenate([jnp.pad(w_hi, lane_pad), jnp.pad(w_lo, lane_pad)], axis=1)
    b_pad = jnp.pad(b_router.reshape(1, ne), lane_pad, constant_values=-jnp.inf)
    return pl.pallas_call(
        functools.partial(_ln1_kernel, alpha=alpha),
        grid=(t // tm,),
        in_specs=[
            pl.BlockSpec((tm, d), row),
            pl.BlockSpec((d, d), const),
            pl.BlockSpec((tm, d), row),
            pl.BlockSpec((1, N_MOD, d), lambda i: (i // per, 0, 0)),
            pl.BlockSpec((1, d), const),
            pl.BlockSpec((1, d), const),
            pl.BlockSpec((d, 2 * LANES_V7X), const),
            pl.BlockSpec((1, LANES_V7X), const),
        ],
        out_specs=[
            pl.BlockSpec((tm, d), row),
            pl.BlockSpec((tm, d), row),
            pl.BlockSpec((tm, LANES_V7X), row),
            pl.BlockSpec((tm, LANES_V7X), row),
        ],
        out_shape=[
            jax.ShapeDtypeStruct((t, d), F32),
            jax.ShapeDtypeStruct((t, d), F32),
            jax.ShapeDtypeStruct((t, LANES_V7X), jnp.int32),
            jax.ShapeDtypeStruct((t, LANES_V7X), F32),
        ],
        compiler_params=_params("parallel"),
        name="oproj_ln1_router",
    )(merged, w_o, x2d, mod3, ln_g.reshape(1, d), ln_b.reshape(1, d), w_split, b_pad)


def _start_row_gather(src_hbm, idx_ref, dst_ref, sem, dst_slot):
    for r in range(dst_ref.shape[1]):
        pltpu.make_async_copy(src_hbm.at[pl.ds(idx_ref[0, 0, r], 1), :],
                              dst_ref.at[dst_slot, pl.ds(r, 1), :], sem.at[dst_slot]).start()


def _moe_kernel(be_ref, used_ref, tok_ref, tok_next_ref, g_ref, h_hbm, wg_ref, wu_ref, bg_ref, bu_ref, wd_ref, bd_ref,
                o_ref, xg_ref, xb_ref, acc_ref, sem):
    del be_ref
    i = pl.program_id(0)
    f = pl.program_id(1)
    nblk = pl.num_programs(0)
    bm = xg_ref.shape[1]
    used = used_ref[i] > 0
    slot = i % 2

    @pl.when(jnp.logical_and(f == 0, jnp.logical_and(i == 0, used)))
    def _():
        _start_row_gather(h_hbm, tok_ref, xg_ref, sem, 0)

    @pl.when(jnp.logical_and(f == 0, used))
    def _():
        pltpu.make_async_copy(h_hbm.at[pl.ds(0, bm), :], xg_ref.at[slot], sem.at[slot]).wait()
        xb_ref[...] = xg_ref[slot].astype(xb_ref.dtype)
        acc_ref[...] = jnp.zeros(acc_ref.shape, F32)

    nxt = jnp.minimum(i + 1, nblk - 1)

    prefetch = jnp.logical_and(f == 0, jnp.logical_and(i + 1 < nblk, used_ref[nxt] > 0))
    for next_slot in (0, 1):
        @pl.when(jnp.logical_and(prefetch, slot == 1 - next_slot))
        def _():
            _start_row_gather(h_hbm, tok_next_ref, xg_ref, sem, next_slot)

    @pl.when(used)
    def _():
        x = xb_ref[...]
        gate = jnp.dot(x, wg_ref[0], preferred_element_type=F32) + bg_ref[0]
        up = jnp.dot(x, wu_ref[0], preferred_element_type=F32) + bu_ref[0]
        gate = jnp.minimum(gate, SWIGLU_LIMIT)
        up = jnp.clip(up, -SWIGLU_LIMIT, SWIGLU_LIMIT)
        act = (up + 1.0) * gate * _sigmoid(SWIGLU_ALPHA * gate)
        acc_ref[...] += jnp.dot(act.astype(MXU_DTYPE), wd_ref[0], preferred_element_type=F32)

    last = f == pl.num_programs(1) - 1

    @pl.when(jnp.logical_and(last, used))
    def _():
        o_ref[...] = (acc_ref[...] + bd_ref[0]) * g_ref[...]

    @pl.when(jnp.logical_and(last, jnp.logical_not(used)))
    def _():
        o_ref[...] = jnp.zeros(o_ref.shape, F32)


def _moe(h2, blk_e, blk_used, slot_tok, slot_g, w_gu, b_gu, w_down, b_down, bm):
    t, d = h2.shape
    ne, _, two_ff = w_gu.shape
    dff = two_ff // 2
    tf = _tile(dff, TF_MOE)
    nf = dff // tf
    p = slot_tok.shape[0]
    nblk = p // bm

    def ftile(i, f, bu):
        return jnp.where(bu[i] > 0, f, nf - 1)

    grid_spec = pltpu.PrefetchScalarGridSpec(
        num_scalar_prefetch=2,
        grid=(nblk, nf),
        in_specs=[
            pl.BlockSpec((1, 1, bm), lambda i, f, be, bu: (i, 0, 0), memory_space=pltpu.SMEM),
            pl.BlockSpec((1, 1, bm), lambda i, f, be, bu: (jnp.minimum(i + 1, nblk - 1), 0, 0),
                         memory_space=pltpu.SMEM),
            pl.BlockSpec((bm, 1), lambda i, f, be, bu: (i, 0)),
            pl.BlockSpec(memory_space=pl.ANY),
            pl.BlockSpec((1, d, tf), lambda i, f, be, bu: (be[i], 0, ftile(i, f, bu))),
            pl.BlockSpec((1, d, tf), lambda i, f, be, bu: (be[i], 0, nf + ftile(i, f, bu))),
            pl.BlockSpec((1, 1, tf), lambda i, f, be, bu: (be[i], 0, ftile(i, f, bu))),
            pl.BlockSpec((1, 1, tf), lambda i, f, be, bu: (be[i], 0, nf + ftile(i, f, bu))),
            pl.BlockSpec((1, tf, d), lambda i, f, be, bu: (be[i], ftile(i, f, bu), 0)),
            pl.BlockSpec((1, 1, d), lambda i, f, be, bu: (be[i], 0, 0)),
        ],
        out_specs=pl.BlockSpec((bm, d), lambda i, f, be, bu: (i, 0)),
        scratch_shapes=[
            pltpu.VMEM((2, bm, d), F32),
            pltpu.VMEM((bm, d), MXU_DTYPE),
            pltpu.VMEM((bm, d), F32),
            pltpu.SemaphoreType.DMA((2,)),
        ],
    )
    slot_tok3 = slot_tok.reshape(nblk, 1, bm)
    return pl.pallas_call(
        _moe_kernel,
        grid_spec=grid_spec,
        out_shape=jax.ShapeDtypeStruct((p, d), F32),
        compiler_params=_params("arbitrary", "arbitrary"),
        name="moe_experts",
    )(blk_e, blk_used, slot_tok3, slot_tok3, slot_g.reshape(p, 1), h2,
      w_gu, w_gu, b_gu.reshape(ne, 1, two_ff), b_gu.reshape(ne, 1, two_ff), w_down, b_down.reshape(ne, 1, d))


def _final_kernel(dst_ref, dst_next_ref, x1_ref, mod_ref, lng_ref, lnb_ref, yb_hbm, o_ref, buf_ref, sem, *, alpha):
    i = pl.program_id(0)
    nt = pl.num_programs(0)
    tm = x1_ref.shape[0]
    nrow = buf_ref.shape[1]
    slot = i % 2

    @pl.when(i == 0)
    def _():
        _start_row_gather(yb_hbm, dst_ref, buf_ref, sem, 0)

    for next_slot in (0, 1):
        @pl.when(jnp.logical_and(i + 1 < nt, slot == 1 - next_slot))
        def _():
            _start_row_gather(yb_hbm, dst_next_ref, buf_ref, sem, next_slot)

    pltpu.make_async_copy(yb_hbm.at[pl.ds(0, nrow), :], buf_ref.at[slot], sem.at[slot]).wait()
    y = buf_ref[slot, 0:tm, :]
    for k in range(1, TOP_K):
        y = y + buf_ref[slot, k * tm:(k + 1) * tm, :]
    g2 = mod_ref[0, 5:6, :]
    o_ref[...] = _layer_norm_rows(alpha * x1_ref[...] + g2 * y, lng_ref[...], lnb_ref[...])


def _combine_ln2(dst, x1, mod3, ln_g, ln_b, yb, seq, alpha):
    t, d = x1.shape
    tm = _tile(seq, TM_FINAL)
    per = seq // tm
    nt = t // tm
    dst_tiles = dst.reshape(nt, tm, TOP_K).transpose(0, 2, 1).reshape(nt, 1, TOP_K * tm)
    return pl.pallas_call(
        functools.partial(_final_kernel, alpha=alpha),
        grid=(nt,),
        in_specs=[
            pl.BlockSpec((1, 1, TOP_K * tm), lambda i: (i, 0, 0), memory_space=pltpu.SMEM),
            pl.BlockSpec((1, 1, TOP_K * tm), lambda i: (jnp.minimum(i + 1, nt - 1), 0, 0), memory_space=pltpu.SMEM),
            pl.BlockSpec((tm, d), lambda i: (i, 0)),
            pl.BlockSpec((1, N_MOD, d), lambda i: (i // per, 0, 0)),
            pl.BlockSpec((1, d), lambda i: (0, 0)),
            pl.BlockSpec((1, d), lambda i: (0, 0)),
            pl.BlockSpec(memory_space=pl.ANY),
        ],
        out_specs=pl.BlockSpec((tm, d), lambda i: (i, 0)),
        out_shape=jax.ShapeDtypeStruct((t, d), F32),
        scratch_shapes=[
            pltpu.VMEM((2, TOP_K * tm, d), F32),
            pltpu.SemaphoreType.DMA((2,)),
        ],
        compiler_params=_params("arbitrary"),
        name="combine_ln2",
    )(dst_tiles, dst_tiles, x1, mod3, ln_g.reshape(1, d), ln_b.reshape(1, d), yb)


def _route_slots(top_i, gates, n_experts, bm):
    t = top_i.shape[0]
    na = t * TOP_K
    flat_e = top_i.reshape(-1)
    order = jnp.argsort(flat_e).astype(jnp.int32)
    rank = jnp.argsort(order).astype(jnp.int32)
    experts = jnp.arange(n_experts, dtype=jnp.int32)
    counts = jnp.sum((flat_e[:, None] == experts[None, :]).astype(jnp.int32), axis=0)
    starts = jnp.cumsum(counts) - counts
    padded = ((counts + bm - 1) // bm) * bm
    pends = jnp.cumsum(padded)
    pstarts = pends - padded
    dest = (pstarts[flat_e] + rank - starts[flat_e]).reshape(t, TOP_K)
    p = na + n_experts * bm
    nblk = p // bm
    slot = jnp.arange(p, dtype=jnp.int32)
    blk_start = jnp.arange(nblk, dtype=jnp.int32) * bm
    blk_e = jnp.minimum(jnp.sum((blk_start[:, None] >= pends[None, :]).astype(jnp.int32), axis=1), n_experts - 1)
    slot_e = jnp.repeat(blk_e, bm)
    within = slot - pstarts[slot_e]
    real = within < counts[slot_e]
    assign = order[jnp.clip(starts[slot_e] + within, 0, na - 1)]
    slot_tok = jnp.where(real, assign // TOP_K, 0)
    slot_g = jnp.where(real, gates.reshape(-1)[assign], 0.0)
    blk_used = (blk_start < pends[-1]).astype(jnp.int32)
    return slot_tok, slot_g, dest, blk_e, blk_used


def _pad_rope_cols(w):
    half = ROPE_B // 2
    z = jnp.zeros(w.shape[:-1] + (LANES_V7X // 2 - half,), w.dtype)
    return jnp.concatenate([w[..., :half], z, w[..., half:], z], axis=-1)


def _rope_tables(seq):
    half = ROPE_B // 2
    pos = jnp.arange(seq, dtype=F32)
    freqs = ROPE_THETA ** (-jnp.arange(half, dtype=F32) / half)
    ang = pos[:, None] * freqs[None, :]
    cos = jnp.cos(ang)
    sin = jnp.sin(ang)
    z = jnp.zeros((seq, LANES_V7X // 2 - half), F32)
    return (jnp.concatenate([cos, z, cos, z], axis=1), jnp.concatenate([-sin, z, sin, z], axis=1))


def _prepare_layer(w_in, w_uq, w_ukv, w_a_out, w_b_out, w_o, w_gu, w_down):
    d = w_in.shape[0]
    qa_w = N_Q_A * HD_A
    ka_w = N_KV_A * HD_A
    q_lora = w_uq.shape[0]
    kv_lora = w_ukv.shape[0]
    assert q_lora == kv_lora
    group = N_Q_A // N_KV_A
    offs = [0]
    for wdt in (qa_w, ka_w, ka_w, q_lora, kv_lora, ROPE_B, d, d):
        offs.append(offs[-1] + wdt)
    assert offs[-1] == w_in.shape[1]
    part = lambda i: w_in[:, offs[i]:offs[i + 1]]
    w_qa = part(0).reshape(d, N_KV_A, group, HD_A).transpose(0, 2, 1, 3).reshape(d, qa_w) * (HD_A ** -0.5 * LOG2_E)
    cast = lambda w: w.astype(MXU_DTYPE)
    w_uq3 = w_uq.reshape(q_lora, N_H_B, NOPE_B + ROPE_B)
    w_uq_p = jnp.concatenate([w_uq3[..., :NOPE_B], _pad_rope_cols(w_uq3[..., NOPE_B:])], axis=-1)
    w_ukv3 = w_ukv.reshape(kv_lora, N_H_B, NOPE_B + V_B)
    return dict(
        w_qkv=cast(jnp.concatenate([w_qa, part(1), part(2)], axis=1)),
        w_lat=cast(jnp.concatenate([part(3), part(4)], axis=1)),
        w_kr=cast(_pad_rope_cols(part(5))),
        w_gates=cast(jnp.concatenate([part(6), part(7)], axis=1)),
        w_uq=cast(w_uq_p.reshape(q_lora, N_H_B * (NOPE_B + LANES_V7X))),
        w_uk=cast(w_ukv3[..., :NOPE_B].reshape(kv_lora, N_H_B * NOPE_B)),
        w_uv_t=cast(w_ukv3[..., NOPE_B:].reshape(kv_lora, N_H_B * V_B).T),
        w_a_out=cast(w_a_out.reshape(N_KV_A, group, HD_A, d).transpose(1, 0, 2, 3).reshape(qa_w, d)),
        w_b_out=cast(w_b_out),
        w_o=cast(w_o),
        w_gu=cast(w_gu),
        w_down=cast(w_down),
    )


def _encoder_layer(x, mod, wts, sinks_a, q_norm_g, kv_norm_g, ln1_g, ln1_b, w_router, b_router,
                   b_gu, b_down, ln2_g, ln2_b, alpha):
    b, s, d = x.shape
    t = b * s
    x2d = x.reshape(t, d)
    mod3 = mod.reshape(b, N_MOD, d)
    tables = _rope_tables(s)

    h = _modulate(x2d, mod3, s)
    qkv = _matmul(h, wts['w_qkv'], MXU_DTYPE, "proj_qkv_a")
    lat = _matmul(h, wts['w_lat'], F32, "proj_latents")
    gates = _matmul(h, wts['w_gates'], MXU_DTYPE, "proj_gates")
    kr = _matmul_rope(h, wts['w_kr'], tables[0], tables[1], s, "proj_k_rope")

    oa = _window_attention(qkv, sinks_a, b, s)

    qscale = (NOPE_B + ROPE_B) ** -0.5 * LOG2_E
    q = _q_up(lat, q_norm_g, wts['w_uq'], tables, s, qscale)
    kcat = _k_up(lat, kv_norm_g, wts['w_uk'], kr, s)
    vt = _v_up_t(lat, kv_norm_g, wts['w_uv_t'], _tile(s, TKC_MLA))
    ob = _mla_attention(q, kcat, vt, b, s)

    merged = _merge(oa, ob, wts['w_a_out'], wts['w_b_out'], gates)
    x1, h2, ridx, rgate = _oproj_ln1_router(merged, wts['w_o'], x2d, mod3, ln1_g, ln1_b, w_router, b_router, s, alpha)

    n_experts = w_router.shape[1]
    bm = _tile(t, BM_MOE)
    slot_tok, slot_g, dest, blk_e, blk_used = _route_slots(ridx[:, :TOP_K], rgate[:, :TOP_K], n_experts, bm)
    yb = _moe(h2, blk_e, blk_used, slot_tok, slot_g, wts['w_gu'], b_gu, wts['w_down'], b_down, bm)
    out = _combine_ln2(dest, x1, mod3, ln2_g, ln2_b, yb, s, alpha)
    return out.reshape(b, s, d)


def kernel(x_prompt, x_sample, c_prompt, c_sample, w_ada, b_ada, w_in, sinks_a, q_norm_g, kv_norm_g, w_uq, w_ukv, w_a_out, w_b_out, w_o, ln1_g, ln1_b, w_router, b_router, w_gu, b_gu, w_down, b_down, ln2_g, ln2_b):
    depth = w_ada.shape[0]
    alpha = (2.0 * depth) ** 0.25
    nbp = c_prompt.shape[0]
    nbs = c_sample.shape[0]
    sub = 8
    pad = (-(nbp + nbs)) % sub
    y_prompt, y_sample = x_prompt, x_sample
    for l in range(depth):
        c_all = jnp.concatenate([c_prompt, c_sample, jnp.zeros((pad, c_prompt.shape[1]), F32)], axis=0)
        mod = _ada(c_all, w_ada[l], b_ada[l])
        wts = _prepare_layer(w_in[l], w_uq[l], w_ukv[l], w_a_out[l], w_b_out[l], w_o[l], w_gu[l], w_down[l])
        rest = (wts, sinks_a[l], q_norm_g[l], kv_norm_g[l], ln1_g[l], ln1_b[l], w_router[l], b_router[l],
                b_gu[l], b_down[l], ln2_g[l], ln2_b[l], alpha)
        y_prompt = _encoder_layer(y_prompt, mod[:nbp], *rest)
        y_sample = _encoder_layer(y_sample, mod[nbp:nbp + nbs], *rest)
    return (y_prompt, y_sample)
```

```python
import functools

import jax
import jax.numpy as jnp
from jax import lax
from jax.experimental import pallas as pl
from jax.experimental.pallas import tpu as pltpu

N_Q_A = 16
N_KV_A = 2
HD_A = 64
WINDOW = 128
N_H_B = 16
NOPE_B = 128
ROPE_B = 64
V_B = 128
ROPE_THETA = 10000.0
TOP_K = 4
SWIGLU_LIMIT = 7.0
SWIGLU_ALPHA = 1.702
LN_EPS = 1e-5
RMS_EPS = 1e-6
N_MOD = 6
LOG2_E = 1.4426950408889634

LANES_V7X = 128
SUBLANES_V7X = 8
VMEM_LIMIT_BYTES_V7X = 56 * 1024 * 1024

MXU_DTYPE = jnp.bfloat16
F32 = jnp.float32

TM_PROJ = 1024
TN_PROJ = 512
TM_LN = 256
TQ_MLA = 1024
TKC_MLA = 512
BM_MOE = 512
TF_MOE = 512
TM_FINAL = 128
TN_ADA = 1024


def _tile(n, pref):
    if n <= pref:
        return n
    align = LANES_V7X if pref >= LANES_V7X else SUBLANES_V7X
    t = pref - pref % align
    while t > 0 and n % t:
        t -= align
    assert t > 0, (n, pref)
    return t


def _params(*sem):
    return pltpu.CompilerParams(dimension_semantics=sem, vmem_limit_bytes=VMEM_LIMIT_BYTES_V7X)


def _sigmoid(x):
    return 1.0 / (1.0 + jnp.exp(-x))


def _rope_pairs(x, cos, sin):
    return x * cos + pltpu.roll(x, LANES_V7X // 2, axis=1) * sin


def _ada_kernel(c_ref, w_ref, b_ref, o_ref):
    c = c_ref[...]
    a = (c * _sigmoid(c)).astype(MXU_DTYPE)
    o_ref[...] = jnp.dot(a, w_ref[...].astype(MXU_DTYPE), preferred_element_type=F32) + b_ref[...]


def _ada(c, w_ada, b_ada):
    bp, d = c.shape
    n = w_ada.shape[1]
    tn = _tile(n, TN_ADA)
    return pl.pallas_call(
        _ada_kernel,
        grid=(n // tn,),
        in_specs=[
            pl.BlockSpec((bp, d), lambda j: (0, 0)),
            pl.BlockSpec((d, tn), lambda j: (0, j)),
            pl.BlockSpec((1, tn), lambda j: (0, j)),
        ],
        out_specs=pl.BlockSpec((bp, tn), lambda j: (0, j)),
        out_shape=jax.ShapeDtypeStruct((bp, n), F32),
        compiler_params=_params("parallel"),
        name="ada",
    )(c, w_ada, b_ada.reshape(1, n))


def _mod_kernel(x_ref, mod_ref, o_ref):
    sh = mod_ref[0, 0:1, :]
    sc = mod_ref[0, 1:2, :]
    o_ref[...] = (x_ref[...] * (1.0 + sc) + sh).astype(o_ref.dtype)


def _modulate(x2d, mod3, seq):
    t, d = x2d.shape
    tm = _tile(seq, TM_PROJ)
    per = seq // tm
    return pl.pallas_call(
        _mod_kernel,
        grid=(t // tm,),
        in_specs=[
            pl.BlockSpec((tm, d), lambda i: (i, 0)),
            pl.BlockSpec((1, N_MOD, d), lambda i: (i // per, 0, 0)),
        ],
        out_specs=pl.BlockSpec((tm, d), lambda i: (i, 0)),
        out_shape=jax.ShapeDtypeStruct((t, d), MXU_DTYPE),
        compiler_params=_params("parallel"),
        name="modulate1",
    )(x2d, mod3)


def _mm_kernel(x_ref, w_ref, o_ref):
    o_ref[...] = jnp.dot(x_ref[...], w_ref[...], preferred_element_type=F32).astype(o_ref.dtype)


def _matmul(x, w, out_dtype, name):
    m, k = x.shape
    n = w.shape[1]
    tm = _tile(m, TM_PROJ)
    tn = _tile(n, TN_PROJ)
    return pl.pallas_call(
        _mm_kernel,
        grid=(m // tm, n // tn),
        in_specs=[
            pl.BlockSpec((tm, k), lambda i, j: (i, 0)),
            pl.BlockSpec((k, tn), lambda i, j: (0, j)),
        ],
        out_specs=pl.BlockSpec((tm, tn), lambda i, j: (i, j)),
        out_shape=jax.ShapeDtypeStruct((m, n), out_dtype),
        compiler_params=_params("parallel", "parallel"),
        name=name,
    )(x, w)


def _mm_rope_kernel(x_ref, w_ref, cos_ref, sin_ref, o_ref):
    acc = jnp.dot(x_ref[...], w_ref[...], preferred_element_type=F32)
    o_ref[...] = _rope_pairs(acc, cos_ref[...], sin_ref[...]).astype(o_ref.dtype)


def _matmul_rope(x, w, cos, sin, seq, name):
    m, k = x.shape
    n = w.shape[1]
    assert n == LANES_V7X
    tm = _tile(seq, TM_PROJ)
    per = seq // tm
    return pl.pallas_call(
        _mm_rope_kernel,
        grid=(m // tm,),
        in_specs=[
            pl.BlockSpec((tm, k), lambda i: (i, 0)),
            pl.BlockSpec((k, n), lambda i: (0, 0)),
            pl.BlockSpec((tm, n), lambda i: (i % per, 0)),
            pl.BlockSpec((tm, n), lambda i: (i % per, 0)),
        ],
        out_specs=pl.BlockSpec((tm, n), lambda i: (i, 0)),
        out_shape=jax.ShapeDtypeStruct((m, n), MXU_DTYPE),
        compiler_params=_params("parallel"),
        name=name,
    )(x, w, cos, sin)


def _rms_norm_to(xn_ref, x_ref, g_ref):
    @pl.when(pl.program_id(1) == 0)
    def _():
        xf = x_ref[...]
        r = lax.rsqrt(jnp.mean(xf * xf, axis=-1, keepdims=True) + RMS_EPS)
        xn_ref[...] = ((xf * r) * g_ref[...]).astype(xn_ref.dtype)


def _q_up_kernel(x_ref, g_ref, w_ref, cos_ref, sin_ref, o_ref, xn_ref, *, scale):
    _rms_norm_to(xn_ref, x_ref, g_ref)
    acc = jnp.dot(xn_ref[...], w_ref[...], preferred_element_type=F32) * scale
    cos = cos_ref[...]
    sin = sin_ref[...]
    head_w = NOPE_B + LANES_V7X
    pieces = []
    for c in range(acc.shape[1] // head_w):
        pieces.append(acc[:, c * head_w:c * head_w + NOPE_B])
        pieces.append(_rope_pairs(acc[:, c * head_w + NOPE_B:(c + 1) * head_w], cos, sin))
    o_ref[...] = jnp.concatenate(pieces, axis=1).astype(o_ref.dtype)


def _k_up_kernel(x_ref, g_ref, w_ref, kr_ref, o_ref, xn_ref):
    _rms_norm_to(xn_ref, x_ref, g_ref)
    acc = jnp.dot(xn_ref[...], w_ref[...], preferred_element_type=F32).astype(o_ref.dtype)
    kr = kr_ref[...]
    pieces = []
    for c in range(acc.shape[1] // NOPE_B):
        pieces.append(acc[:, c * NOPE_B:(c + 1) * NOPE_B])
        pieces.append(kr)
    o_ref[...] = jnp.concatenate(pieces, axis=1)


def _v_up_t_kernel(x_ref, g_ref, wt_ref, o_ref, xn_ref):
    _rms_norm_to(xn_ref, x_ref, g_ref)
    acc = lax.dot_general(wt_ref[...], xn_ref[...], (((1,), (1,)), ((), ())), preferred_element_type=F32)
    o_ref[0] = acc.reshape(o_ref.shape[1:]).astype(o_ref.dtype)


def _latent_up(kernel_fn, lat, col_block, gain, w, extra, extra_specs, out_spec, out_shape, tm, grid_n, w_spec, name):
    m = lat.shape[0]
    k = gain.shape[0]
    return pl.pallas_call(
        kernel_fn,
        grid=(m // tm, grid_n),
        in_specs=[
            pl.BlockSpec((tm, k), lambda i, j: (i, col_block)),
            pl.BlockSpec((1, k), lambda i, j: (0, 0)),
            w_spec,
        ] + extra_specs,
        out_specs=out_spec,
        out_shape=out_shape,
        scratch_shapes=[pltpu.VMEM((tm, k), MXU_DTYPE)],
        compiler_params=_params("parallel", "arbitrary"),
        name=name,
    )(lat, gain.reshape(1, k), w, *extra)


def _q_up(lat, gain, w, tables, seq, scale):
    m = lat.shape[0]
    k, n = w.shape
    tm = _tile(seq, TM_PROJ)
    tn = _tile(n, max(TN_PROJ, NOPE_B + LANES_V7X))
    per = seq // tm
    table_spec = pl.BlockSpec((tm, LANES_V7X), lambda i, j: (i % per, 0))
    return _latent_up(
        functools.partial(_q_up_kernel, scale=scale), lat, 0, gain, w, list(tables), [table_spec, table_spec],
        pl.BlockSpec((tm, tn), lambda i, j: (i, j)), jax.ShapeDtypeStruct((m, n), MXU_DTYPE),
        tm, n // tn, pl.BlockSpec((k, tn), lambda i, j: (0, j)), "q_up")


def _k_up(lat, gain, w, kr, seq):
    m = lat.shape[0]
    k, n = w.shape
    tm = _tile(seq, TM_PROJ)
    tn = _tile(n, TN_PROJ)
    return _latent_up(
        _k_up_kernel, lat, 1, gain, w, [kr], [pl.BlockSpec((tm, LANES_V7X), lambda i, j: (i, 0))],
        pl.BlockSpec((tm, 2 * tn), lambda i, j: (i, j)), jax.ShapeDtypeStruct((m, 2 * n), MXU_DTYPE),
        tm, n // tn, pl.BlockSpec((k, tn), lambda i, j: (0, j)), "k_up")


def _v_up_t(lat, gain, wt, chunk):
    m = lat.shape[0]
    n, k = wt.shape
    tn = _tile(n, TN_PROJ)
    heads_per = tn // V_B
    return _latent_up(
        _v_up_t_kernel, lat, 1, gain, wt, [], [],
        pl.BlockSpec((1, heads_per, V_B, chunk), lambda i, j: (i, j, 0, 0)),
        jax.ShapeDtypeStruct((m // chunk, n // V_B, V_B, chunk), MXU_DTYPE),
        chunk, n // tn, pl.BlockSpec((tn, k), lambda i, j: (j, 0)), "v_up_t")


def _window_kernel(q_ref, kp_ref, ko_ref, kn_ref, vp_ref, vo_ref, vn_ref, bias_ref, sink_ref, o_ref):
    w = WINDOW
    group = N_Q_A // N_KV_A
    q = q_ref[...]
    k3 = jnp.concatenate([kp_ref[...], ko_ref[...], kn_ref[...]], axis=0)
    v3 = jnp.concatenate([vp_ref[...], vo_ref[...], vn_ref[...]], axis=0)
    lane = lax.broadcasted_iota(jnp.int32, k3.shape, 1)
    lo = lane < HD_A
    zero = jnp.zeros_like(k3)
    kbd = jnp.concatenate([jnp.where(lo, k3, zero), jnp.where(lo, zero, k3)], axis=0)
    vbd = jnp.concatenate([jnp.where(lo, v3, zero), jnp.where(lo, zero, v3)], axis=0)
    vbd_t = vbd.astype(F32).T.astype(MXU_DTYPE)
    out_lo = lax.broadcasted_iota(jnp.int32, (2 * HD_A, w), 0) < HD_A

    outs = []
    for g in range(group):
        qg = q[:, g * 2 * HD_A:(g + 1) * 2 * HD_A]
        s = lax.dot_general(kbd, qg, (((1,), (1,)), ((), ())), preferred_element_type=F32) + bias_ref[0, g]
        s0 = s[:3 * w]
        s1 = s[3 * w:]
        sink0 = sink_ref[0:1, g:g + 1] * LOG2_E
        sink1 = sink_ref[0:1, group + g:group + g + 1] * LOG2_E
        m0 = jnp.maximum(jnp.max(s0, axis=0, keepdims=True), sink0)
        m1 = jnp.maximum(jnp.max(s1, axis=0, keepdims=True), sink1)
        p0 = jnp.exp2(s0 - m0)
        p1 = jnp.exp2(s1 - m1)
        d0 = jnp.sum(p0, axis=0, keepdims=True) + jnp.exp2(sink0 - m0)
        d1 = jnp.sum(p1, axis=0, keepdims=True) + jnp.exp2(sink1 - m1)
        p = jnp.concatenate([p0, p1], axis=0).astype(MXU_DTYPE)
        o_t = jnp.dot(vbd_t, p, preferred_element_type=F32)
        outs.append(o_t * jnp.where(out_lo, 1.0 / d0, 1.0 / d1))
    o_ref[...] = jnp.concatenate(outs, axis=0).T.astype(o_ref.dtype)


def _window_bias_tables():
    w = WINDOW
    group = N_Q_A // N_KV_A
    key = jnp.arange(6 * w)
    head = key // (3 * w)
    krel = key % (3 * w) - w
    dist = jnp.abs(krel[:, None] - jnp.arange(w)[None, :])
    slopes = jnp.exp2(-8.0 * jnp.arange(1, N_Q_A + 1, dtype=F32) / N_Q_A).reshape(N_KV_A, group)
    slope = slopes[head, :].T
    alibi = -slope[:, :, None] * dist.astype(F32)[None] * LOG2_E
    tables = []
    for case in range(4):
        ok = dist <= w
        if case & 1:
            ok = ok & (krel >= 0)[:, None]
        if case & 2:
            ok = ok & (krel < w)[:, None]
        tables.append(jnp.where(ok[None], alibi, -jnp.inf))
    return jnp.stack(tables)


def _window_attention(qkv, sinks, batch, seq):
    t = qkv.shape[0]
    w = WINDOW
    nb = seq // w
    qw = N_Q_A * HD_A
    kvw = N_KV_A * HD_A
    group = N_Q_A // N_KV_A
    assert N_KV_A == 2 and kvw == LANES_V7X
    kblk = qw // kvw
    vblk = kblk + 1

    def kv_spec(col, off):
        return pl.BlockSpec((w, kvw), lambda b, n: (b * nb + jnp.clip(n + off, 0, nb - 1), col))

    def edge_case(b, n):
        return ((n == 0).astype(jnp.int32) + 2 * (n == nb - 1).astype(jnp.int32), 0, 0, 0)

    return pl.pallas_call(
        _window_kernel,
        grid=(batch, nb),
        in_specs=[
            pl.BlockSpec((w, qw), lambda b, n: (b * nb + n, 0)),
            kv_spec(kblk, -1), kv_spec(kblk, 0), kv_spec(kblk, 1),
            kv_spec(vblk, -1), kv_spec(vblk, 0), kv_spec(vblk, 1),
            pl.BlockSpec((1, group, 6 * w, w), edge_case),
            pl.BlockSpec((1, N_Q_A), lambda b, n: (0, 0)),
        ],
        out_specs=pl.BlockSpec((w, qw), lambda b, n: (b * nb + n, 0)),
        out_shape=jax.ShapeDtypeStruct((t, qw), MXU_DTYPE),
        compiler_params=_params("parallel", "parallel"),
        name="window_attention",
    )(qkv, qkv, qkv, qkv, qkv, qkv, qkv, _window_bias_tables(), sinks.reshape(1, N_Q_A))


def _mla_kernel(q_ref, k_ref, vt_ref, o_ref, s0_ref, s1_ref, m_ref, l_ref, acc_ref, *, nchunk):
    tkc = s0_ref.shape[0]
    q = q_ref[...]
    m_ref[...] = jnp.full(m_ref.shape, -jnp.inf, F32)
    l_ref[...] = jnp.zeros(l_ref.shape, F32)
    acc_ref[...] = jnp.zeros(acc_ref.shape, F32)

    def scores(c, s_ref):
        start = pl.multiple_of(c * tkc, tkc)
        s_ref[...] = lax.dot_general(k_ref[pl.ds(start, tkc), :], q, (((1,), (1,)), ((), ())),
                                     preferred_element_type=F32)

    def accumulate(c, s_ref):
        s = s_ref[...]
        m_old = m_ref[...]
        m_new = jnp.maximum(m_old, jnp.max(s, axis=0, keepdims=True))
        alpha = jnp.exp2(m_old - m_new)
        p = jnp.exp2(s - m_new)
        l_ref[...] = alpha * l_ref[...] + jnp.sum(p, axis=0, keepdims=True)
        acc_ref[...] = alpha * acc_ref[...] + jnp.dot(vt_ref[c], p.astype(MXU_DTYPE), preferred_element_type=F32)
        m_ref[...] = m_new

    scores(0, s0_ref)

    def body(j, carry):
        c = 2 * j
        scores(c + 1, s1_ref)
        accumulate(c, s0_ref)
        scores(c + 2, s0_ref)
        accumulate(c + 1, s1_ref)
        return carry

    lax.fori_loop(0, nchunk // 2 - 1, body, 0)
    scores(nchunk - 1, s1_ref)
    accumulate(nchunk - 2, s0_ref)
    accumulate(nchunk - 1, s1_ref)
    o_ref[...] = (acc_ref[...] * (1.0 / l_ref[...])).T.astype(o_ref.dtype)


def _mla_attention(q, kcat, vt, batch, seq):
    t = q.shape[0]
    tkc = vt.shape[-1]
    nchunk = seq // tkc
    assert nchunk % 2 == 0
    tq = _tile(seq, TQ_MLA)
    nq = seq // tq
    qh = NOPE_B + LANES_V7X
    assert NOPE_B == LANES_V7X and V_B == LANES_V7X
    vt5 = vt.reshape(batch, nchunk, N_H_B, V_B, tkc)
    return pl.pallas_call(
        functools.partial(_mla_kernel, nchunk=nchunk),
        grid=(batch, N_H_B, nq),
        in_specs=[
            pl.BlockSpec((tq, qh), lambda b, h, qi: (b * nq + qi, h)),
            pl.BlockSpec((seq, qh), lambda b, h, qi: (b, h)),
            pl.BlockSpec((None, nchunk, None, V_B, tkc), lambda b, h, qi: (b, 0, h, 0, 0)),
        ],
        out_specs=pl.BlockSpec((tq, V_B), lambda b, h, qi: (b * nq + qi, h)),
        out_shape=jax.ShapeDtypeStruct((t, N_H_B * V_B), MXU_DTYPE),
        scratch_shapes=[
            pltpu.VMEM((tkc, tq), F32),
            pltpu.VMEM((tkc, tq), F32),
            pltpu.VMEM((1, tq), F32),
            pltpu.VMEM((1, tq), F32),
            pltpu.VMEM((V_B, tq), F32),
        ],
        compiler_params=_params("parallel", "parallel", "parallel"),
        name="mla_attention",
    )(q, kcat, vt5)


def _merge_kernel(oa_ref, ob_ref, wa_ref, wb_ref, ga_ref, gb_ref, o_ref):
    a = jnp.dot(oa_ref[...], wa_ref[...], preferred_element_type=F32)
    b = jnp.dot(ob_ref[...], wb_ref[...], preferred_element_type=F32)
    ga = ga_ref[...].astype(F32)
    gb = gb_ref[...].astype(F32)
    o_ref[...] = (_sigmoid(ga) * a + _sigmoid(gb) * b).astype(o_ref.dtype)


def _merge(oa, ob, wa, wb, gates):
    t = oa.shape[0]
    d = wa.shape[1]
    tm = _tile(t, TM_PROJ)
    tn = _tile(d, TN_PROJ)
    nj = d // tn
    return pl.pallas_call(
        _merge_kernel,
        grid=(t // tm, nj),
        in_specs=[
            pl.BlockSpec((tm, oa.shape[1]), lambda i, j: (i, 0)),
            pl.BlockSpec((tm, ob.shape[1]), lambda i, j: (i, 0)),
            pl.BlockSpec((wa.shape[0], tn), lambda i, j: (0, j)),
            pl.BlockSpec((wb.shape[0], tn), lambda i, j: (0, j)),
            pl.BlockSpec((tm, tn), lambda i, j: (i, j)),
            pl.BlockSpec((tm, tn), lambda i, j: (i, nj + j)),
        ],
        out_specs=pl.BlockSpec((tm, tn), lambda i, j: (i, j)),
        out_shape=jax.ShapeDtypeStruct((t, d), MXU_DTYPE),
        compiler_params=_params("parallel", "parallel"),
        name="merge",
    )(oa, ob, wa, wb, gates, gates)


def _layer_norm_rows(z, g, b):
    mu = jnp.mean(z, axis=-1, keepdims=True)
    zc = z - mu
    var = jnp.mean(zc * zc, axis=-1, keepdims=True)
    return (zc * lax.rsqrt(var + LN_EPS)) * g + b


def _ln1_kernel(mg_ref, wo_ref, x_ref, mod_ref, lng_ref, lnb_ref, wr_ref, br_ref,
                x1_ref, h2_ref, ri_ref, rg_ref, *, alpha):
    y = jnp.dot(mg_ref[...], wo_ref[...], preferred_element_type=F32)
    g1 = mod_ref[0, 2:3, :]
    sh2 = mod_ref[0, 3:4, :]
    sc2 = mod_ref[0, 4:5, :]
    x1 = _layer_norm_rows(alpha * x_ref[...] + g1 * y, lng_ref[...], lnb_ref[...])
    x1_ref[...] = x1
    h2 = x1 * (1.0 + sc2) + sh2
    h2_ref[...] = h2
    h_hi = h2.astype(MXU_DTYPE)
    h_lo = (h2 - h_hi.astype(F32)).astype(MXU_DTYPE)
    r_hi = jnp.dot(h_hi, wr_ref[...], preferred_element_type=F32)
    r_lo = jnp.dot(h_lo, wr_ref[:, :LANES_V7X], preferred_element_type=F32)
    logits = r_hi[:, :LANES_V7X] + r_hi[:, LANES_V7X:] + r_lo + br_ref[...]
    ne = logits.shape[1]
    eio = lax.broadcasted_iota(jnp.int32, logits.shape, 1).astype(F32)
    vals, idxs = [], []
    for _ in range(TOP_K):
        mx = jnp.max(logits, axis=1, keepdims=True)
        ix = jnp.min(jnp.where(logits == mx, eio, float(ne)), axis=1, keepdims=True)
        vals.append(mx)
        idxs.append(ix)
        logits = jnp.where(eio == ix, -jnp.inf, logits)
    es = [jnp.exp(v - vals[0]) for v in vals]
    tot = es[0]
    for e in es[1:]:
        tot = tot + e
    lane = lax.broadcasted_iota(jnp.int32, ri_ref.shape, 1)
    ri = jnp.zeros(ri_ref.shape, F32)
    rg = jnp.zeros(rg_ref.shape, F32)
    for k in range(TOP_K):
        ri = jnp.where(lane == k, idxs[k], ri)
        rg = jnp.where(lane == k, es[k] / tot, rg)
    ri_ref[...] = ri.astype(jnp.int32)
    rg_ref[...] = rg


def _oproj_ln1_router(merged, w_o, x2d, mod3, ln_g, ln_b, w_router, b_router, seq, alpha):
    t, d = x2d.shape
    ne = w_router.shape[1]
    assert ne <= LANES_V7X
    tm = _tile(seq, TM_LN)
    per = seq // tm
    row = lambda i: (i, 0)
    const = lambda i: (0, 0)
    w_hi = w_router.astype(MXU_DTYPE)
    w_lo = (w_router - w_hi.astype(F32)).astype(MXU_DTYPE)
    lane_pad = ((0, 0), (0, LANES_V7X - ne))
    w_split = jnp.concatenate([jnp.pad(w_hi, lane_pad), jnp.pad(w_lo, lane_pad)], axis=1)
    b_pad = jnp.pad(b_router.reshape(1, ne), lane_pad, constant_values=-jnp.inf)
    return pl.pallas_call(
        functools.partial(_ln1_kernel, alpha=alpha),
        grid=(t // tm,),
        in_specs=[
            pl.BlockSpec((tm, d), row),
            pl.BlockSpec((d, d), const),
            pl.BlockSpec((tm, d), row),
            pl.BlockSpec((1, N_MOD, d), lambda i: (i // per, 0, 0)),
            pl.BlockSpec((1, d), const),
            pl.BlockSpec((1, d), const),
            pl.BlockSpec((d, 2 * LANES_V7X), const),
            pl.BlockSpec((1, LANES_V7X), const),
        ],
        out_specs=[
            pl.BlockSpec((tm, d), row),
            pl.BlockSpec((tm, d), row),
            pl.BlockSpec((tm, LANES_V7X), row),
            pl.BlockSpec((tm, LANES_V7X), row),
        ],
        out_shape=[
            jax.ShapeDtypeStruct((t, d), F32),
            jax.ShapeDtypeStruct((t, d), F32),
            jax.ShapeDtypeStruct((t, LANES_V7X), jnp.int32),
            jax.ShapeDtypeStruct((t, LANES_V7X), F32),
        ],
        compiler_params=_params("parallel"),
        name="oproj_ln1_router",
    )(merged, w_o, x2d, mod3, ln_g.reshape(1, d), ln_b.reshape(1, d), w_split, b_pad)


def _start_row_gather(src_hbm, idx_ref, dst_ref, sem, dst_slot):
    for r in range(dst_ref.shape[1]):
        pltpu.make_async_copy(src_hbm.at[pl.ds(idx_ref[0, 0, r], 1), :],
                              dst_ref.at[dst_slot, pl.ds(r, 1), :], sem.at[dst_slot]).start()


def _moe_kernel(be_ref, used_ref, tok_ref, tok_next_ref, g_ref, h_hbm, wg_ref, wu_ref, bg_ref, bu_ref, wd_ref, bd_ref,
                o_ref, xg_ref, acc_ref, sem, *, nf):
    del be_ref
    i = pl.program_id(0)
    f = pl.program_id(1)
    nblk = pl.num_programs(0)
    bm = xg_ref.shape[1]
    used = used_ref[i] > 0
    slot = i % 2

    @pl.when(jnp.logical_and(f == 0, jnp.logical_and(i == 0, used)))
    def _():
        _start_row_gather(h_hbm, tok_ref, xg_ref, sem, 0)

    @pl.when(jnp.logical_and(f == 0, used))
    def _():
        pltpu.make_async_copy(h_hbm.at[pl.ds(0, bm), :], xg_ref.at[slot], sem.at[slot]).wait()

    nxt = jnp.minimum(i + 1, nblk - 1)

    prefetch = jnp.logical_and(f == 0, jnp.logical_and(i + 1 < nblk, used_ref[nxt] > 0))
    for next_slot in (0, 1):
        @pl.when(jnp.logical_and(prefetch, slot == 1 - next_slot))
        def _():
            _start_row_gather(h_hbm, tok_next_ref, xg_ref, sem, next_slot)

    def down_contribution():
        x = xg_ref[slot].astype(MXU_DTYPE)
        gate = jnp.dot(x, wg_ref[0], preferred_element_type=F32) + bg_ref[0]
        up = jnp.dot(x, wu_ref[0], preferred_element_type=F32) + bu_ref[0]
        gate = jnp.minimum(gate, SWIGLU_LIMIT)
        up = jnp.clip(up, -SWIGLU_LIMIT, SWIGLU_LIMIT)
        act = (up + 1.0) * gate * _sigmoid(SWIGLU_ALPHA * gate)
        return jnp.dot(act.astype(MXU_DTYPE), wd_ref[0], preferred_element_type=F32)

    first = f == 0
    last = f == nf - 1

    if nf > 1:
        @pl.when(jnp.logical_and(used, first))
        def _():
            acc_ref[...] = down_contribution()

        @pl.when(jnp.logical_and(used, jnp.logical_and(jnp.logical_not(first), jnp.logical_not(last))))
        def _():
            acc_ref[...] += down_contribution()

    @pl.when(jnp.logical_and(used, last))
    def _():
        total = down_contribution()
        if nf > 1:
            total = total + acc_ref[...]
        o_ref[...] = (total + bd_ref[0]) * g_ref[...]

    @pl.when(jnp.logical_and(last, jnp.logical_not(used)))
    def _():
        o_ref[...] = jnp.zeros(o_ref.shape, F32)


def _moe(h2, blk_e, blk_used, slot_tok, slot_g, w_gu, b_gu, w_down, b_down, bm):
    t, d = h2.shape
    ne, _, two_ff = w_gu.shape
    dff = two_ff // 2
    tf = _tile(dff, TF_MOE)
    nf = dff // tf
    p = slot_tok.shape[0]
    nblk = p // bm

    def ftile(i, f, bu):
        return jnp.where(bu[i] > 0, f, nf - 1)

    grid_spec = pltpu.PrefetchScalarGridSpec(
        num_scalar_prefetch=2,
        grid=(nblk, nf),
        in_specs=[
            pl.BlockSpec((1, 1, bm), lambda i, f, be, bu: (i, 0, 0), memory_space=pltpu.SMEM),
            pl.BlockSpec((1, 1, bm), lambda i, f, be, bu: (jnp.minimum(i + 1, nblk - 1), 0, 0),
                         memory_space=pltpu.SMEM),
            pl.BlockSpec((bm, 1), lambda i, f, be, bu: (i, 0)),
            pl.BlockSpec(memory_space=pl.ANY),
            pl.BlockSpec((1, d, tf), lambda i, f, be, bu: (be[i], 0, ftile(i, f, bu))),
            pl.BlockSpec((1, d, tf), lambda i, f, be, bu: (be[i], 0, nf + ftile(i, f, bu))),
            pl.BlockSpec((1, 1, tf), lambda i, f, be, bu: (be[i], 0, ftile(i, f, bu))),
            pl.BlockSpec((1, 1, tf), lambda i, f, be, bu: (be[i], 0, nf + ftile(i, f, bu))),
            pl.BlockSpec((1, tf, d), lambda i, f, be, bu: (be[i], ftile(i, f, bu), 0)),
            pl.BlockSpec((1, 1, d), lambda i, f, be, bu: (be[i], 0, 0)),
        ],
        out_specs=pl.BlockSpec((bm, d), lambda i, f, be, bu: (i, 0)),
        scratch_shapes=[
            pltpu.VMEM((2, bm, d), F32),
            pltpu.VMEM((bm, d), F32),
            pltpu.SemaphoreType.DMA((2,)),
        ],
    )
    slot_tok3 = slot_tok.reshape(nblk, 1, bm)
    return pl.pallas_call(
        functools.partial(_moe_kernel, nf=nf),
        grid_spec=grid_spec,
        out_shape=jax.ShapeDtypeStruct((p, d), F32),
        compiler_params=_params("arbitrary", "arbitrary"),
        name="moe_experts",
    )(blk_e, blk_used, slot_tok3, slot_tok3, slot_g.reshape(p, 1), h2,
      w_gu, w_gu, b_gu.reshape(ne, 1, two_ff), b_gu.reshape(ne, 1, two_ff), w_down, b_down.reshape(ne, 1, d))


def _final_kernel(dst_ref, dst_next_ref, x1_ref, mod_ref, lng_ref, lnb_ref, yb_hbm, o_ref, buf_ref, sem, *, alpha):
    i = pl.program_id(0)
    nt = pl.num_programs(0)
    tm = x1_ref.shape[0]
    nrow = buf_ref.shape[1]
    slot = i % 2

    @pl.when(i == 0)
    def _():
        _start_row_gather(yb_hbm, dst_ref, buf_ref, sem, 0)

    for next_slot in (0, 1):
        @pl.when(jnp.logical_and(i + 1 < nt, slot == 1 - next_slot))
        def _():
            _start_row_gather(yb_hbm, dst_next_ref, buf_ref, sem, next_slot)

    pltpu.make_async_copy(yb_hbm.at[pl.ds(0, nrow), :], buf_ref.at[slot], sem.at[slot]).wait()
    y = buf_ref[slot, 0:tm, :]
    for k in range(1, TOP_K):
        y = y + buf_ref[slot, k * tm:(k + 1) * tm, :]
    g2 = mod_ref[0, 5:6, :]
    o_ref[...] = _layer_norm_rows(alpha * x1_ref[...] + g2 * y, lng_ref[...], lnb_ref[...])


def _combine_ln2(dst, x1, mod3, ln_g, ln_b, yb, seq, alpha):
    t, d = x1.shape
    tm = _tile(seq, TM_FINAL)
    per = seq // tm
    nt = t // tm
    dst_tiles = dst.reshape(nt, tm, TOP_K).transpose(0, 2, 1).reshape(nt, 1, TOP_K * tm)
    return pl.pallas_call(
        functools.partial(_final_kernel, alpha=alpha),
        grid=(nt,),
        in_specs=[
            pl.BlockSpec((1, 1, TOP_K * tm), lambda i: (i, 0, 0), memory_space=pltpu.SMEM),
            pl.BlockSpec((1, 1, TOP_K * tm), lambda i: (jnp.minimum(i + 1, nt - 1), 0, 0), memory_space=pltpu.SMEM),
            pl.BlockSpec((tm, d), lambda i: (i, 0)),
            pl.BlockSpec((1, N_MOD, d), lambda i: (i // per, 0, 0)),
            pl.BlockSpec((1, d), lambda i: (0, 0)),
            pl.BlockSpec((1, d), lambda i: (0, 0)),
            pl.BlockSpec(memory_space=pl.ANY),
        ],
        out_specs=pl.BlockSpec((tm, d), lambda i: (i, 0)),
        out_shape=jax.ShapeDtypeStruct((t, d), F32),
        scratch_shapes=[
            pltpu.VMEM((2, TOP_K * tm, d), F32),
            pltpu.SemaphoreType.DMA((2,)),
        ],
        compiler_params=_params("arbitrary"),
        name="combine_ln2",
    )(dst_tiles, dst_tiles, x1, mod3, ln_g.reshape(1, d), ln_b.reshape(1, d), yb)


def _route_slots(top_i, gates, n_experts, bm):
    t = top_i.shape[0]
    na = t * TOP_K
    flat_e = top_i.reshape(-1)
    order = jnp.argsort(flat_e).astype(jnp.int32)
    rank = jnp.argsort(order).astype(jnp.int32)
    experts = jnp.arange(n_experts, dtype=jnp.int32)
    onehot = experts[:, None] == flat_e[None, :]
    counts = jnp.sum(onehot.astype(jnp.int32), axis=1)
    starts = jnp.cumsum(counts) - counts
    padded = ((counts + bm - 1) // bm) * bm
    pends = jnp.cumsum(padded)
    pstarts = pends - padded
    shift = jnp.sum(jnp.where(onehot, (pstarts - starts)[:, None], 0), axis=0)
    dest = (shift + rank).reshape(t, TOP_K)
    p = na + n_experts * bm
    nblk = p // bm
    blk_start = jnp.arange(nblk, dtype=jnp.int32) * bm
    blk_e = jnp.minimum(jnp.sum((blk_start[:, None] >= pends[None, :]).astype(jnp.int32), axis=1), n_experts - 1)
    blk_within = blk_start - pstarts[blk_e]
    row = jnp.arange(bm, dtype=jnp.int32)[None, :]
    within = blk_within[:, None] + row
    real = (within < counts[blk_e][:, None]).reshape(p)
    assign = order[jnp.clip(starts[blk_e][:, None] + within, 0, na - 1).reshape(p)]
    slot_tok = jnp.where(real, assign // TOP_K, 0)
    slot_g = jnp.where(real, gates.reshape(-1)[assign], 0.0)
    blk_used = (blk_start < pends[-1]).astype(jnp.int32)
    return slot_tok, slot_g, dest, blk_e, blk_used


def _pad_rope_cols(w):
    half = ROPE_B // 2
    z = jnp.zeros(w.shape[:-1] + (LANES_V7X // 2 - half,), w.dtype)
    return jnp.concatenate([w[..., :half], z, w[..., half:], z], axis=-1)


def _rope_tables(seq):
    half = ROPE_B // 2
    pos = jnp.arange(seq, dtype=F32)
    freqs = ROPE_THETA ** (-jnp.arange(half, dtype=F32) / half)
    ang = pos[:, None] * freqs[None, :]
    cos = jnp.cos(ang)
    sin = jnp.sin(ang)
    z = jnp.zeros((seq, LANES_V7X // 2 - half), F32)
    return (jnp.concatenate([cos, z, cos, z], axis=1), jnp.concatenate([-sin, z, sin, z], axis=1))


def _prepare_layer(w_in, w_uq, w_ukv, w_a_out, w_b_out, w_o, w_gu, w_down):
    d = w_in.shape[0]
    qa_w = N_Q_A * HD_A
    ka_w = N_KV_A * HD_A
    q_lora = w_uq.shape[0]
    kv_lora = w_ukv.shape[0]
    assert q_lora == kv_lora
    group = N_Q_A // N_KV_A
    offs = [0]
    for wdt in (qa_w, ka_w, ka_w, q_lora, kv_lora, ROPE_B, d, d):
        offs.append(offs[-1] + wdt)
    assert offs[-1] == w_in.shape[1]
    part = lambda i: w_in[:, offs[i]:offs[i + 1]]
    w_qa = part(0).reshape(d, N_KV_A, group, HD_A).transpose(0, 2, 1, 3).reshape(d, qa_w) * (HD_A ** -0.5 * LOG2_E)
    cast = lambda w: w.astype(MXU_DTYPE)
    w_uq3 = w_uq.reshape(q_lora, N_H_B, NOPE_B + ROPE_B)
    w_uq_p = jnp.concatenate([w_uq3[..., :NOPE_B], _pad_rope_cols(w_uq3[..., NOPE_B:])], axis=-1)
    w_ukv3 = w_ukv.reshape(kv_lora, N_H_B, NOPE_B + V_B)
    return dict(
        w_qkv=cast(jnp.concatenate([w_qa, part(1), part(2)], axis=1)),
        w_lat=cast(jnp.concatenate([part(3), part(4)], axis=1)),
        w_kr=cast(_pad_rope_cols(part(5))),
        w_gates=cast(jnp.concatenate([part(6), part(7)], axis=1)),
        w_uq=cast(w_uq_p.reshape(q_lora, N_H_B * (NOPE_B + LANES_V7X))),
        w_uk=cast(w_ukv3[..., :NOPE_B].reshape(kv_lora, N_H_B * NOPE_B)),
        w_uv_t=cast(w_ukv3[..., NOPE_B:].reshape(kv_lora, N_H_B * V_B).T),
        w_a_out=cast(w_a_out.reshape(N_KV_A, group, HD_A, d).transpose(1, 0, 2, 3).reshape(qa_w, d)),
        w_b_out=cast(w_b_out),
        w_o=cast(w_o),
        w_gu=cast(w_gu),
        w_down=cast(w_down),
    )


def _encoder_layer(x, mod, wts, sinks_a, q_norm_g, kv_norm_g, ln1_g, ln1_b, w_router, b_router,
                   b_gu, b_down, ln2_g, ln2_b, alpha):
    b, s, d = x.shape
    t = b * s
    x2d = x.reshape(t, d)
    mod3 = mod.reshape(b, N_MOD, d)
    tables = _rope_tables(s)

    h = _modulate(x2d, mod3, s)
    qkv = _matmul(h, wts['w_qkv'], MXU_DTYPE, "proj_qkv_a")
    lat = _matmul(h, wts['w_lat'], F32, "proj_latents")
    gates = _matmul(h, wts['w_gates'], MXU_DTYPE, "proj_gates")
    kr = _matmul_rope(h, wts['w_kr'], tables[0], tables[1], s, "proj_k_rope")

    oa = _window_attention(qkv, sinks_a, b, s)

    qscale = (NOPE_B + ROPE_B) ** -0.5 * LOG2_E
    q = _q_up(lat, q_norm_g, wts['w_uq'], tables, s, qscale)
    kcat = _k_up(lat, kv_norm_g, wts['w_uk'], kr, s)
    vt = _v_up_t(lat, kv_norm_g, wts['w_uv_t'], _tile(s, TKC_MLA))
    ob = _mla_attention(q, kcat, vt, b, s)

    merged = _merge(oa, ob, wts['w_a_out'], wts['w_b_out'], gates)
    x1, h2, ridx, rgate = _oproj_ln1_router(merged, wts['w_o'], x2d, mod3, ln1_g, ln1_b, w_router, b_router, s, alpha)

    n_experts = w_router.shape[1]
    bm = _tile(t, BM_MOE)
    slot_tok, slot_g, dest, blk_e, blk_used = _route_slots(ridx[:, :TOP_K], rgate[:, :TOP_K], n_experts, bm)
    yb = _moe(h2, blk_e, blk_used, slot_tok, slot_g, wts['w_gu'], b_gu, wts['w_down'], b_down, bm)
    out = _combine_ln2(dest, x1, mod3, ln2_g, ln2_b, yb, s, alpha)
    return out.reshape(b, s, d)


def kernel(x_prompt, x_sample, c_prompt, c_sample, w_ada, b_ada, w_in, sinks_a, q_norm_g, kv_norm_g, w_uq, w_ukv, w_a_out, w_b_out, w_o, ln1_g, ln1_b, w_router, b_router, w_gu, b_gu, w_down, b_down, ln2_g, ln2_b):
    depth = w_ada.shape[0]
    alpha = (2.0 * depth) ** 0.25
    nbp = c_prompt.shape[0]
    nbs = c_sample.shape[0]
    sub = 8
    pad = (-(nbp + nbs)) % sub
    y_prompt, y_sample = x_prompt, x_sample
    for l in range(depth):
        c_all = jnp.concatenate([c_prompt, c_sample, jnp.zeros((pad, c_prompt.shape[1]), F32)], axis=0)
        mod = _ada(c_all, w_ada[l], b_ada[l])
        wts = _prepare_layer(w_in[l], w_uq[l], w_ukv[l], w_a_out[l], w_b_out[l], w_o[l], w_gu[l], w_down[l])
        rest = (wts, sinks_a[l], q_norm_g[l], kv_norm_g[l], ln1_g[l], ln1_b[l], w_router[l], b_router[l],
                b_gu[l], b_down[l], ln2_g[l], ln2_b[l], alpha)
        y_prompt = _encoder_layer(y_prompt, mod[:nbp], *rest)
        y_sample = _encoder_layer(y_sample, mod[nbp:nbp + nbs], *rest)
    return (y_prompt, y_sample)
```

```python
import functools

import jax
import jax.numpy as jnp
from jax import lax
from jax.experimental import pallas as pl
from jax.experimental.pallas import tpu as pltpu

N_Q_A = 16
N_KV_A = 2
HD_A = 64
WINDOW = 128
N_H_B = 16
NOPE_B = 128
ROPE_B = 64
V_B = 128
ROPE_THETA = 10000.0
TOP_K = 4
SWIGLU_LIMIT = 7.0
SWIGLU_ALPHA = 1.702
LN_EPS = 1e-5
RMS_EPS = 1e-6
N_MOD = 6
LOG2_E = 1.4426950408889634

LANES_V7X = 128
SUBLANES_V7X = 8
VMEM_LIMIT_BYTES_V7X = 56 * 1024 * 1024

MXU_DTYPE = jnp.bfloat16
F32 = jnp.float32

TM_PROJ = 1024
TN_PROJ = 512
TM_LN = 256
TQ_MLA = 1024
TKC_MLA = 512
BM_MOE = 512
TF_MOE = 512
TM_FINAL = 128
TN_ADA = 1024


def _tile(n, pref):
    if n <= pref:
        return n
    align = LANES_V7X if pref >= LANES_V7X else SUBLANES_V7X
    t = pref - pref % align
    while t > 0 and n % t:
        t -= align
    assert t > 0, (n, pref)
    return t


def _params(*sem):
    return pltpu.CompilerParams(dimension_semantics=sem, vmem_limit_bytes=VMEM_LIMIT_BYTES_V7X)


def _sigmoid(x):
    return 1.0 / (1.0 + jnp.exp(-x))


def _rope_pairs(x, cos, sin):
    return x * cos + pltpu.roll(x, LANES_V7X // 2, axis=1) * sin


def _ada_kernel(c_ref, w_ref, b_ref, o_ref):
    c = c_ref[...]
    a = (c * _sigmoid(c)).astype(MXU_DTYPE)
    o_ref[...] = jnp.dot(a, w_ref[...].astype(MXU_DTYPE), preferred_element_type=F32) + b_ref[...]


def _ada(c, w_ada, b_ada):
    bp, d = c.shape
    n = w_ada.shape[1]
    tn = _tile(n, TN_ADA)
    return pl.pallas_call(
        _ada_kernel,
        grid=(n // tn,),
        in_specs=[
            pl.BlockSpec((bp, d), lambda j: (0, 0)),
            pl.BlockSpec((d, tn), lambda j: (0, j)),
            pl.BlockSpec((1, tn), lambda j: (0, j)),
        ],
        out_specs=pl.BlockSpec((bp, tn), lambda j: (0, j)),
        out_shape=jax.ShapeDtypeStruct((bp, n), F32),
        compiler_params=_params("parallel"),
        name="ada",
    )(c, w_ada, b_ada.reshape(1, n))


def _mod_kernel(x_ref, mod_ref, o_ref):
    sh = mod_ref[0, 0:1, :]
    sc = mod_ref[0, 1:2, :]
    o_ref[...] = (x_ref[...] * (1.0 + sc) + sh).astype(o_ref.dtype)


def _modulate(x2d, mod3, seq):
    t, d = x2d.shape
    tm = _tile(seq, TM_PROJ)
    per = seq // tm
    return pl.pallas_call(
        _mod_kernel,
        grid=(t // tm,),
        in_specs=[
            pl.BlockSpec((tm, d), lambda i: (i, 0)),
            pl.BlockSpec((1, N_MOD, d), lambda i: (i // per, 0, 0)),
        ],
        out_specs=pl.BlockSpec((tm, d), lambda i: (i, 0)),
        out_shape=jax.ShapeDtypeStruct((t, d), MXU_DTYPE),
        compiler_params=_params("parallel"),
        name="modulate1",
    )(x2d, mod3)


def _mm_kernel(x_ref, w_ref, o_ref):
    o_ref[...] = jnp.dot(x_ref[...], w_ref[...], preferred_element_type=F32).astype(o_ref.dtype)


def _matmul(x, w, out_dtype, name):
    m, k = x.shape
    n = w.shape[1]
    tm = _tile(m, TM_PROJ)
    tn = _tile(n, TN_PROJ)
    return pl.pallas_call(
        _mm_kernel,
        grid=(m // tm, n // tn),
        in_specs=[
            pl.BlockSpec((tm, k), lambda i, j: (i, 0)),
            pl.BlockSpec((k, tn), lambda i, j: (0, j)),
        ],
        out_specs=pl.BlockSpec((tm, tn), lambda i, j: (i, j)),
        out_shape=jax.ShapeDtypeStruct((m, n), out_dtype),
        compiler_params=_params("parallel", "parallel"),
        name=name,
    )(x, w)


def _mm_rope_kernel(x_ref, w_ref, cos_ref, sin_ref, o_ref):
    acc = jnp.dot(x_ref[...], w_ref[...], preferred_element_type=F32)
    o_ref[...] = _rope_pairs(acc, cos_ref[...], sin_ref[...]).astype(o_ref.dtype)


def _matmul_rope(x, w, cos, sin, seq, name):
    m, k = x.shape
    n = w.shape[1]
    assert n == LANES_V7X
    tm = _tile(seq, TM_PROJ)
    per = seq // tm
    return pl.pallas_call(
        _mm_rope_kernel,
        grid=(m // tm,),
        in_specs=[
            pl.BlockSpec((tm, k), lambda i: (i, 0)),
            pl.BlockSpec((k, n), lambda i: (0, 0)),
            pl.BlockSpec((tm, n), lambda i: (i % per, 0)),
            pl.BlockSpec((tm, n), lambda i: (i % per, 0)),
        ],
        out_specs=pl.BlockSpec((tm, n), lambda i: (i, 0)),
        out_shape=jax.ShapeDtypeStruct((m, n), MXU_DTYPE),
        compiler_params=_params("parallel"),
        name=name,
    )(x, w, cos, sin)


def _rms_norm_to(xn_ref, x_ref, g_ref):
    @pl.when(pl.program_id(1) == 0)
    def _():
        xf = x_ref[...]
        r = lax.rsqrt(jnp.mean(xf * xf, axis=-1, keepdims=True) + RMS_EPS)
        xn_ref[...] = ((xf * r) * g_ref[...]).astype(xn_ref.dtype)


def _q_up_kernel(x_ref, g_ref, w_ref, cos_ref, sin_ref, o_ref, xn_ref, *, scale):
    _rms_norm_to(xn_ref, x_ref, g_ref)
    acc = jnp.dot(xn_ref[...], w_ref[...], preferred_element_type=F32) * scale
    cos = cos_ref[...]
    sin = sin_ref[...]
    head_w = NOPE_B + LANES_V7X
    pieces = []
    for c in range(acc.shape[1] // head_w):
        pieces.append(acc[:, c * head_w:c * head_w + NOPE_B])
        pieces.append(_rope_pairs(acc[:, c * head_w + NOPE_B:(c + 1) * head_w], cos, sin))
    o_ref[...] = jnp.concatenate(pieces, axis=1).astype(o_ref.dtype)


def _k_up_kernel(x_ref, g_ref, w_ref, kr_ref, o_ref, xn_ref):
    _rms_norm_to(xn_ref, x_ref, g_ref)
    acc = jnp.dot(xn_ref[...], w_ref[...], preferred_element_type=F32).astype(o_ref.dtype)
    kr = kr_ref[...]
    pieces = []
    for c in range(acc.shape[1] // NOPE_B):
        pieces.append(acc[:, c * NOPE_B:(c + 1) * NOPE_B])
        pieces.append(kr)
    o_ref[...] = jnp.concatenate(pieces, axis=1)


def _v_up_t_kernel(x_ref, g_ref, wt_ref, o_ref, xn_ref):
    _rms_norm_to(xn_ref, x_ref, g_ref)
    acc = lax.dot_general(wt_ref[...], xn_ref[...], (((1,), (1,)), ((), ())), preferred_element_type=F32)
    o_ref[0] = acc.reshape(o_ref.shape[1:]).astype(o_ref.dtype)


def _latent_up(kernel_fn, lat, col_block, gain, w, extra, extra_specs, out_spec, out_shape, tm, grid_n, w_spec, name):
    m = lat.shape[0]
    k = gain.shape[0]
    return pl.pallas_call(
        kernel_fn,
        grid=(m // tm, grid_n),
        in_specs=[
            pl.BlockSpec((tm, k), lambda i, j: (i, col_block)),
            pl.BlockSpec((1, k), lambda i, j: (0, 0)),
            w_spec,
        ] + extra_specs,
        out_specs=out_spec,
        out_shape=out_shape,
        scratch_shapes=[pltpu.VMEM((tm, k), MXU_DTYPE)],
        compiler_params=_params("parallel", "arbitrary"),
        name=name,
    )(lat, gain.reshape(1, k), w, *extra)


def _q_up(lat, gain, w, tables, seq, scale):
    m = lat.shape[0]
    k, n = w.shape
    tm = _tile(seq, TM_PROJ)
    tn = _tile(n, max(TN_PROJ, NOPE_B + LANES_V7X))
    per = seq // tm
    table_spec = pl.BlockSpec((tm, LANES_V7X), lambda i, j: (i % per, 0))
    return _latent_up(
        functools.partial(_q_up_kernel, scale=scale), lat, 0, gain, w, list(tables), [table_spec, table_spec],
        pl.BlockSpec((tm, tn), lambda i, j: (i, j)), jax.ShapeDtypeStruct((m, n), MXU_DTYPE),
        tm, n // tn, pl.BlockSpec((k, tn), lambda i, j: (0, j)), "q_up")


def _k_up(lat, gain, w, kr, seq):
    m = lat.shape[0]
    k, n = w.shape
    tm = _tile(seq, TM_PROJ)
    tn = _tile(n, TN_PROJ)
    return _latent_up(
        _k_up_kernel, lat, 1, gain, w, [kr], [pl.BlockSpec((tm, LANES_V7X), lambda i, j: (i, 0))],
        pl.BlockSpec((tm, 2 * tn), lambda i, j: (i, j)), jax.ShapeDtypeStruct((m, 2 * n), MXU_DTYPE),
        tm, n // tn, pl.BlockSpec((k, tn), lambda i, j: (0, j)), "k_up")


def _v_up_t(lat, gain, wt, chunk):
    m = lat.shape[0]
    n, k = wt.shape
    tn = _tile(n, TN_PROJ)
    heads_per = tn // V_B
    return _latent_up(
        _v_up_t_kernel, lat, 1, gain, wt, [], [],
        pl.BlockSpec((1, heads_per, V_B, chunk), lambda i, j: (i, j, 0, 0)),
        jax.ShapeDtypeStruct((m // chunk, n // V_B, V_B, chunk), MXU_DTYPE),
        chunk, n // tn, pl.BlockSpec((tn, k), lambda i, j: (j, 0)), "v_up_t")


def _window_kernel(q_ref, kp_ref, ko_ref, kn_ref, vp_ref, vo_ref, vn_ref, bias_ref, sink_ref, o_ref):
    w = WINDOW
    group = N_Q_A // N_KV_A
    q = q_ref[...]
    k3 = jnp.concatenate([kp_ref[...], ko_ref[...], kn_ref[...]], axis=0)
    v3 = jnp.concatenate([vp_ref[...], vo_ref[...], vn_ref[...]], axis=0)
    lane = lax.broadcasted_iota(jnp.int32, k3.shape, 1)
    lo = lane < HD_A
    zero = jnp.zeros_like(k3)
    kbd = jnp.concatenate([jnp.where(lo, k3, zero), jnp.where(lo, zero, k3)], axis=0)
    vbd = jnp.concatenate([jnp.where(lo, v3, zero), jnp.where(lo, zero, v3)], axis=0)
    vbd_t = vbd.astype(F32).T.astype(MXU_DTYPE)
    out_lo = lax.broadcasted_iota(jnp.int32, (2 * HD_A, w), 0) < HD_A

    outs = []
    for g in range(group):
        qg = q[:, g * 2 * HD_A:(g + 1) * 2 * HD_A]
        s = lax.dot_general(kbd, qg, (((1,), (1,)), ((), ())), preferred_element_type=F32) + bias_ref[0, g]
        s0 = s[:3 * w]
        s1 = s[3 * w:]
        sink0 = sink_ref[0:1, g:g + 1] * LOG2_E
        sink1 = sink_ref[0:1, group + g:group + g + 1] * LOG2_E
        m0 = jnp.maximum(jnp.max(s0, axis=0, keepdims=True), sink0)
        m1 = jnp.maximum(jnp.max(s1, axis=0, keepdims=True), sink1)
        p0 = jnp.exp2(s0 - m0)
        p1 = jnp.exp2(s1 - m1)
        d0 = jnp.sum(p0, axis=0, keepdims=True) + jnp.exp2(sink0 - m0)
        d1 = jnp.sum(p1, axis=0, keepdims=True) + jnp.exp2(sink1 - m1)
        p = jnp.concatenate([p0, p1], axis=0).astype(MXU_DTYPE)
        o_t = jnp.dot(vbd_t, p, preferred_element_type=F32)
        outs.append(o_t * jnp.where(out_lo, 1.0 / d0, 1.0 / d1))
    o_ref[...] = jnp.concatenate(outs, axis=0).T.astype(o_ref.dtype)


def _window_bias_tables():
    w = WINDOW
    group = N_Q_A // N_KV_A
    key = jnp.arange(6 * w)
    head = key // (3 * w)
    krel = key % (3 * w) - w
    dist = jnp.abs(krel[:, None] - jnp.arange(w)[None, :])
    slopes = jnp.exp2(-8.0 * jnp.arange(1, N_Q_A + 1, dtype=F32) / N_Q_A).reshape(N_KV_A, group)
    slope = slopes[head, :].T
    alibi = -slope[:, :, None] * dist.astype(F32)[None] * LOG2_E
    tables = []
    for case in range(4):
        ok = dist <= w
        if case & 1:
            ok = ok & (krel >= 0)[:, None]
        if case & 2:
            ok = ok & (krel < w)[:, None]
        tables.append(jnp.where(ok[None], alibi, -jnp.inf))
    return jnp.stack(tables)


def _window_attention(qkv, sinks, batch, seq):
    t = qkv.shape[0]
    w = WINDOW
    nb = seq // w
    qw = N_Q_A * HD_A
    kvw = N_KV_A * HD_A
    group = N_Q_A // N_KV_A
    assert N_KV_A == 2 and kvw == LANES_V7X
    kblk = qw // kvw
    vblk = kblk + 1

    def kv_spec(col, off):
        return pl.BlockSpec((w, kvw), lambda b, n: (b * nb + jnp.clip(n + off, 0, nb - 1), col))

    def edge_case(b, n):
        return ((n == 0).astype(jnp.int32) + 2 * (n == nb - 1).astype(jnp.int32), 0, 0, 0)

    return pl.pallas_call(
        _window_kernel,
        grid=(batch, nb),
        in_specs=[
            pl.BlockSpec((w, qw), lambda b, n: (b * nb + n, 0)),
            kv_spec(kblk, -1), kv_spec(kblk, 0), kv_spec(kblk, 1),
            kv_spec(vblk, -1), kv_spec(vblk, 0), kv_spec(vblk, 1),
            pl.BlockSpec((1, group, 6 * w, w), edge_case),
            pl.BlockSpec((1, N_Q_A), lambda b, n: (0, 0)),
        ],
        out_specs=pl.BlockSpec((w, qw), lambda b, n: (b * nb + n, 0)),
        out_shape=jax.ShapeDtypeStruct((t, qw), MXU_DTYPE),
        compiler_params=_params("parallel", "parallel"),
        name="window_attention",
    )(qkv, qkv, qkv, qkv, qkv, qkv, qkv, _window_bias_tables(), sinks.reshape(1, N_Q_A))


def _mla_kernel(q_ref, k_ref, vt_ref, o_ref, s0_ref, s1_ref, m_ref, l_ref, acc_ref, *, nchunk):
    tkc = s0_ref.shape[0]
    q = q_ref[...]
    m_ref[...] = jnp.full(m_ref.shape, -jnp.inf, F32)
    l_ref[...] = jnp.zeros(l_ref.shape, F32)
    acc_ref[...] = jnp.zeros(acc_ref.shape, F32)

    def scores(c, s_ref):
        start = pl.multiple_of(c * tkc, tkc)
        s_ref[...] = lax.dot_general(k_ref[pl.ds(start, tkc), :], q, (((1,), (1,)), ((), ())),
                                     preferred_element_type=F32)

    def accumulate(c, s_ref):
        s = s_ref[...]
        m_old = m_ref[...]
        m_new = jnp.maximum(m_old, jnp.max(s, axis=0, keepdims=True))
        alpha = jnp.exp2(m_old - m_new)
        p = jnp.exp2(s - m_new)
        l_ref[...] = alpha * l_ref[...] + jnp.sum(p, axis=0, keepdims=True)
        acc_ref[...] = alpha * acc_ref[...] + jnp.dot(vt_ref[c], p.astype(MXU_DTYPE), preferred_element_type=F32)
        m_ref[...] = m_new

    scores(0, s0_ref)

    def body(j, carry):
        c = 2 * j
        scores(c + 1, s1_ref)
        accumulate(c, s0_ref)
        scores(c + 2, s0_ref)
        accumulate(c + 1, s1_ref)
        return carry

    lax.fori_loop(0, nchunk // 2 - 1, body, 0)
    scores(nchunk - 1, s1_ref)
    accumulate(nchunk - 2, s0_ref)
    accumulate(nchunk - 1, s1_ref)
    o_ref[...] = (acc_ref[...] * (1.0 / l_ref[...])).T.astype(o_ref.dtype)


def _mla_attention(q, kcat, vt, batch, seq):
    t = q.shape[0]
    tkc = vt.shape[-1]
    nchunk = seq // tkc
    assert nchunk % 2 == 0
    tq = _tile(seq, TQ_MLA)
    nq = seq // tq
    qh = NOPE_B + LANES_V7X
    assert NOPE_B == LANES_V7X and V_B == LANES_V7X
    vt5 = vt.reshape(batch, nchunk, N_H_B, V_B, tkc)
    return pl.pallas_call(
        functools.partial(_mla_kernel, nchunk=nchunk),
        grid=(batch, N_H_B, nq),
        in_specs=[
            pl.BlockSpec((tq, qh), lambda b, h, qi: (b * nq + qi, h)),
            pl.BlockSpec((seq, qh), lambda b, h, qi: (b, h)),
            pl.BlockSpec((None, nchunk, None, V_B, tkc), lambda b, h, qi: (b, 0, h, 0, 0)),
        ],
        out_specs=pl.BlockSpec((tq, V_B), lambda b, h, qi: (b * nq + qi, h)),
        out_shape=jax.ShapeDtypeStruct((t, N_H_B * V_B), MXU_DTYPE),
        scratch_shapes=[
            pltpu.VMEM((tkc, tq), F32),
            pltpu.VMEM((tkc, tq), F32),
            pltpu.VMEM((1, tq), F32),
            pltpu.VMEM((1, tq), F32),
            pltpu.VMEM((V_B, tq), F32),
        ],
        compiler_params=_params("parallel", "parallel", "parallel"),
        name="mla_attention",
    )(q, kcat, vt5)


def _merge_kernel(oa_ref, ob_ref, wa_ref, wb_ref, ga_ref, gb_ref, o_ref):
    a = jnp.dot(oa_ref[...], wa_ref[...], preferred_element_type=F32)
    b = jnp.dot(ob_ref[...], wb_ref[...], preferred_element_type=F32)
    ga = ga_ref[...].astype(F32)
    gb = gb_ref[...].astype(F32)
    o_ref[...] = (_sigmoid(ga) * a + _sigmoid(gb) * b).astype(o_ref.dtype)


def _merge(oa, ob, wa, wb, gates):
    t = oa.shape[0]
    d = wa.shape[1]
    tm = _tile(t, TM_PROJ)
    tn = _tile(d, TN_PROJ)
    nj = d // tn
    return pl.pallas_call(
        _merge_kernel,
        grid=(t // tm, nj),
        in_specs=[
            pl.BlockSpec((tm, oa.shape[1]), lambda i, j: (i, 0)),
            pl.BlockSpec((tm, ob.shape[1]), lambda i, j: (i, 0)),
            pl.BlockSpec((wa.shape[0], tn), lambda i, j: (0, j)),
            pl.BlockSpec((wb.shape[0], tn), lambda i, j: (0, j)),
            pl.BlockSpec((tm, tn), lambda i, j: (i, j)),
            pl.BlockSpec((tm, tn), lambda i, j: (i, nj + j)),
        ],
        out_specs=pl.BlockSpec((tm, tn), lambda i, j: (i, j)),
        out_shape=jax.ShapeDtypeStruct((t, d), MXU_DTYPE),
        compiler_params=_params("parallel", "parallel"),
        name="merge",
    )(oa, ob, wa, wb, gates, gates)


def _layer_norm_rows(z, g, b):
    mu = jnp.mean(z, axis=-1, keepdims=True)
    zc = z - mu
    var = jnp.mean(zc * zc, axis=-1, keepdims=True)
    return (zc * lax.rsqrt(var + LN_EPS)) * g + b


def _ln1_kernel(mga_ref, mgb_ref, wo_ref, xa_ref, xb_ref, mod_ref, lng_ref, lnb_ref, wr_ref, br_ref,
                x1_ref, h2_ref, ri_ref, rg_ref, *, alpha, tiles_a):
    rest = (wo_ref, mod_ref, lng_ref, lnb_ref, wr_ref, br_ref, x1_ref, h2_ref, ri_ref, rg_ref)
    i = pl.program_id(0)

    @pl.when(i < tiles_a)
    def _():
        _ln1_tile(mga_ref, xa_ref, *rest, alpha=alpha)

    @pl.when(i >= tiles_a)
    def _():
        _ln1_tile(mgb_ref, xb_ref, *rest, alpha=alpha)


def _ln1_tile(mg_ref, x_ref, wo_ref, mod_ref, lng_ref, lnb_ref, wr_ref, br_ref,
              x1_ref, h2_ref, ri_ref, rg_ref, *, alpha):
    y = jnp.dot(mg_ref[...], wo_ref[...], preferred_element_type=F32)
    g1 = mod_ref[0, 2:3, :]
    sh2 = mod_ref[0, 3:4, :]
    sc2 = mod_ref[0, 4:5, :]
    x1 = _layer_norm_rows(alpha * x_ref[...] + g1 * y, lng_ref[...], lnb_ref[...])
    x1_ref[...] = x1
    h2 = x1 * (1.0 + sc2) + sh2
    h2_ref[...] = h2
    h_hi = h2.astype(MXU_DTYPE)
    h_lo = (h2 - h_hi.astype(F32)).astype(MXU_DTYPE)
    r_hi = jnp.dot(h_hi, wr_ref[...], preferred_element_type=F32)
    r_lo = jnp.dot(h_lo, wr_ref[:, :LANES_V7X], preferred_element_type=F32)
    logits = r_hi[:, :LANES_V7X] + r_hi[:, LANES_V7X:] + r_lo + br_ref[...]
    ne = logits.shape[1]
    eio = lax.broadcasted_iota(jnp.int32, logits.shape, 1).astype(F32)
    vals, idxs = [], []
    for _ in range(TOP_K):
        mx = jnp.max(logits, axis=1, keepdims=True)
        ix = jnp.min(jnp.where(logits == mx, eio, float(ne)), axis=1, keepdims=True)
        vals.append(mx)
        idxs.append(ix)
        logits = jnp.where(eio == ix, -jnp.inf, logits)
    es = [jnp.exp(v - vals[0]) for v in vals]
    tot = es[0]
    for e in es[1:]:
        tot = tot + e
    lane = lax.broadcasted_iota(jnp.int32, ri_ref.shape, 1)
    ri = jnp.zeros(ri_ref.shape, F32)
    rg = jnp.zeros(rg_ref.shape, F32)
    for k in range(TOP_K):
        ri = jnp.where(lane == k, idxs[k], ri)
        rg = jnp.where(lane == k, es[k] / tot, rg)
    ri_ref[...] = ri.astype(jnp.int32)
    rg_ref[...] = rg


def _oproj_ln1_router(merged_pair, w_o, x_pair, mod3, ln_g, ln_b, w_router, b_router, seqs, alpha):
    (xa, xb), (mga, mgb) = x_pair, merged_pair
    d = xa.shape[1]
    ne = w_router.shape[1]
    assert ne <= LANES_V7X
    tm = _tile(min(seqs), TM_LN)
    assert all(s % tm == 0 for s in seqs)
    per_a, per_b = seqs[0] // tm, seqs[1] // tm
    tiles_a, tiles_b = xa.shape[0] // tm, xb.shape[0] // tm
    batch_a = xa.shape[0] // seqs[0]
    t = xa.shape[0] + xb.shape[0]
    row = lambda i: (i, 0)
    const = lambda i: (0, 0)
    row_a = lambda i: (jnp.minimum(i, tiles_a - 1), 0)
    row_b = lambda i: (jnp.maximum(i - tiles_a, 0), 0)
    mod_row = lambda i: (jnp.where(i < tiles_a, i // per_a, batch_a + (i - tiles_a) // per_b), 0, 0)
    w_hi = w_router.astype(MXU_DTYPE)
    w_lo = (w_router - w_hi.astype(F32)).astype(MXU_DTYPE)
    lane_pad = ((0, 0), (0, LANES_V7X - ne))
    w_split = jnp.concatenate([jnp.pad(w_hi, lane_pad), jnp.pad(w_lo, lane_pad)], axis=1)
    b_pad = jnp.pad(b_router.reshape(1, ne), lane_pad, constant_values=-jnp.inf)
    return pl.pallas_call(
        functools.partial(_ln1_kernel, alpha=alpha, tiles_a=tiles_a),
        grid=(tiles_a + tiles_b,),
        in_specs=[
            pl.BlockSpec((tm, d), row_a),
            pl.BlockSpec((tm, d), row_b),
            pl.BlockSpec((d, d), const),
            pl.BlockSpec((tm, d), row_a),
            pl.BlockSpec((tm, d), row_b),
            pl.BlockSpec((1, N_MOD, d), mod_row),
            pl.BlockSpec((1, d), const),
            pl.BlockSpec((1, d), const),
            pl.BlockSpec((d, 2 * LANES_V7X), const),
            pl.BlockSpec((1, LANES_V7X), const),
        ],
        out_specs=[
            pl.BlockSpec((tm, d), row),
            pl.BlockSpec((tm, d), row),
            pl.BlockSpec((tm, LANES_V7X), row),
            pl.BlockSpec((tm, LANES_V7X), row),
        ],
        out_shape=[
            jax.ShapeDtypeStruct((t, d), F32),
            jax.ShapeDtypeStruct((t, d), F32),
            jax.ShapeDtypeStruct((t, LANES_V7X), jnp.int32),
            jax.ShapeDtypeStruct((t, LANES_V7X), F32),
        ],
        compiler_params=_params("parallel"),
        name="oproj_ln1_router",
    )(mga, mgb, w_o, xa, xb, mod3, ln_g.reshape(1, d), ln_b.reshape(1, d), w_split, b_pad)


def _start_row_gather(src_hbm, idx_ref, dst_ref, sem, dst_slot):
    for r in range(dst_ref.shape[1]):
        pltpu.make_async_copy(src_hbm.at[pl.ds(idx_ref[0, 0, r], 1), :],
                              dst_ref.at[dst_slot, pl.ds(r, 1), :], sem.at[dst_slot]).start()


def _moe_kernel(be_ref, used_ref, tok_ref, tok_next_ref, g_ref, h_hbm, wg_ref, wu_ref, bg_ref, bu_ref, wd_ref, bd_ref,
                o_ref, xg_ref, acc_ref, sem, *, nf):
    del be_ref
    i = pl.program_id(0)
    f = pl.program_id(1)
    nblk = pl.num_programs(0)
    bm = xg_ref.shape[1]
    used = used_ref[i] > 0
    slot = i % 2

    @pl.when(jnp.logical_and(f == 0, jnp.logical_and(i == 0, used)))
    def _():
        _start_row_gather(h_hbm, tok_ref, xg_ref, sem, 0)

    @pl.when(jnp.logical_and(f == 0, used))
    def _():
        pltpu.make_async_copy(h_hbm.at[pl.ds(0, bm), :], xg_ref.at[slot], sem.at[slot]).wait()

    nxt = jnp.minimum(i + 1, nblk - 1)

    prefetch = jnp.logical_and(f == 0, jnp.logical_and(i + 1 < nblk, used_ref[nxt] > 0))
    for next_slot in (0, 1):
        @pl.when(jnp.logical_and(prefetch, slot == 1 - next_slot))
        def _():
            _start_row_gather(h_hbm, tok_next_ref, xg_ref, sem, next_slot)

    def down_contribution():
        x = xg_ref[slot].astype(MXU_DTYPE)
        gate = jnp.dot(x, wg_ref[0, 0], preferred_element_type=F32) + bg_ref[0]
        up = jnp.dot(x, wu_ref[0, 0], preferred_element_type=F32) + bu_ref[0]
        gate = jnp.minimum(gate, SWIGLU_LIMIT)
        up = jnp.clip(up, -SWIGLU_LIMIT, SWIGLU_LIMIT)
        act = (up + 1.0) * gate * _sigmoid(SWIGLU_ALPHA * gate)
        return jnp.dot(act.astype(MXU_DTYPE), wd_ref[0], preferred_element_type=F32)

    first = f == 0
    last = f == nf - 1

    if nf > 1:
        @pl.when(jnp.logical_and(used, first))
        def _():
            acc_ref[...] = down_contribution()

        @pl.when(jnp.logical_and(used, jnp.logical_and(jnp.logical_not(first), jnp.logical_not(last))))
        def _():
            acc_ref[...] += down_contribution()

    @pl.when(jnp.logical_and(used, last))
    def _():
        total = down_contribution()
        if nf > 1:
            total = total + acc_ref[...]
        o_ref[...] = (total + bd_ref[0]) * g_ref[...]

    @pl.when(jnp.logical_and(last, jnp.logical_not(used)))
    def _():
        o_ref[...] = jnp.zeros(o_ref.shape, F32)


def _moe(h2, blk_e, blk_used, slot_tok, slot_g, w_gu, b_gu, w_down, b_down, bm):
    t, d = h2.shape
    ne, two_nf, _, tf = w_gu.shape
    nf = two_nf // 2
    two_ff = two_nf * tf
    p = slot_tok.shape[0]
    nblk = p // bm

    def ftile(i, f, bu):
        return jnp.where(bu[i] > 0, f, nf - 1)

    grid_spec = pltpu.PrefetchScalarGridSpec(
        num_scalar_prefetch=2,
        grid=(nblk, nf),
        in_specs=[
            pl.BlockSpec((1, 1, bm), lambda i, f, be, bu: (i, 0, 0), memory_space=pltpu.SMEM),
            pl.BlockSpec((1, 1, bm), lambda i, f, be, bu: (jnp.minimum(i + 1, nblk - 1), 0, 0),
                         memory_space=pltpu.SMEM),
            pl.BlockSpec((bm, 1), lambda i, f, be, bu: (i, 0)),
            pl.BlockSpec(memory_space=pl.ANY),
            pl.BlockSpec((1, 1, d, tf), lambda i, f, be, bu: (be[i], ftile(i, f, bu), 0, 0)),
            pl.BlockSpec((1, 1, d, tf), lambda i, f, be, bu: (be[i], nf + ftile(i, f, bu), 0, 0)),
            pl.BlockSpec((1, 1, tf), lambda i, f, be, bu: (be[i], 0, ftile(i, f, bu))),
            pl.BlockSpec((1, 1, tf), lambda i, f, be, bu: (be[i], 0, nf + ftile(i, f, bu))),
            pl.BlockSpec((1, tf, d), lambda i, f, be, bu: (be[i], ftile(i, f, bu), 0)),
            pl.BlockSpec((1, 1, d), lambda i, f, be, bu: (be[i], 0, 0)),
        ],
        out_specs=pl.BlockSpec((bm, d), lambda i, f, be, bu: (i, 0)),
        scratch_shapes=[
            pltpu.VMEM((2, bm, d), F32),
            pltpu.VMEM((bm, d), F32),
            pltpu.SemaphoreType.DMA((2,)),
        ],
    )
    slot_tok3 = slot_tok.reshape(nblk, 1, bm)
    return pl.pallas_call(
        functools.partial(_moe_kernel, nf=nf),
        grid_spec=grid_spec,
        out_shape=jax.ShapeDtypeStruct((p, d), F32),
        compiler_params=_params("arbitrary", "arbitrary"),
        name="moe_experts",
    )(blk_e, blk_used, slot_tok3, slot_tok3, slot_g.reshape(p, 1), h2,
      w_gu, w_gu, b_gu.reshape(ne, 1, two_ff), b_gu.reshape(ne, 1, two_ff), w_down, b_down.reshape(ne, 1, d))


def _final_kernel(dst_ref, dst_next_ref, x1_ref, mod_ref, lng_ref, lnb_ref, yb_hbm, o_ref, buf_ref, sem, *, alpha):
    i = pl.program_id(0)
    nt = pl.num_programs(0)
    tm = x1_ref.shape[0]
    nrow = buf_ref.shape[1]
    slot = i % 2

    @pl.when(i == 0)
    def _():
        _start_row_gather(yb_hbm, dst_ref, buf_ref, sem, 0)

    for next_slot in (0, 1):
        @pl.when(jnp.logical_and(i + 1 < nt, slot == 1 - next_slot))
        def _():
            _start_row_gather(yb_hbm, dst_next_ref, buf_ref, sem, next_slot)

    pltpu.make_async_copy(yb_hbm.at[pl.ds(0, nrow), :], buf_ref.at[slot], sem.at[slot]).wait()
    y = buf_ref[slot, 0:tm, :]
    for k in range(1, TOP_K):
        y = y + buf_ref[slot, k * tm:(k + 1) * tm, :]
    g2 = mod_ref[0, 5:6, :]
    o_ref[...] = _layer_norm_rows(alpha * x1_ref[...] + g2 * y, lng_ref[...], lnb_ref[...])


def _combine_ln2(dst, x1, mod3, ln_g, ln_b, yb, seq, alpha, row0, batch0, rows):
    d = x1.shape[1]
    tm = _tile(seq, TM_FINAL)
    per = seq // tm
    nt = rows // tm
    tile0 = row0 // tm
    assert row0 % tm == 0
    nt_all = dst.shape[0] // tm
    dst_tiles = dst.reshape(nt_all, tm, TOP_K).transpose(0, 2, 1).reshape(nt_all, 1, TOP_K * tm)
    return pl.pallas_call(
        functools.partial(_final_kernel, alpha=alpha),
        grid=(nt,),
        in_specs=[
            pl.BlockSpec((1, 1, TOP_K * tm), lambda i: (tile0 + i, 0, 0), memory_space=pltpu.SMEM),
            pl.BlockSpec((1, 1, TOP_K * tm), lambda i: (tile0 + jnp.minimum(i + 1, nt - 1), 0, 0),
                         memory_space=pltpu.SMEM),
            pl.BlockSpec((tm, d), lambda i: (tile0 + i, 0)),
            pl.BlockSpec((1, N_MOD, d), lambda i: (batch0 + i // per, 0, 0)),
            pl.BlockSpec((1, d), lambda i: (0, 0)),
            pl.BlockSpec((1, d), lambda i: (0, 0)),
            pl.BlockSpec(memory_space=pl.ANY),
        ],
        out_specs=pl.BlockSpec((tm, d), lambda i: (i, 0)),
        out_shape=jax.ShapeDtypeStruct((rows, d), F32),
        scratch_shapes=[
            pltpu.VMEM((2, TOP_K * tm, d), F32),
            pltpu.SemaphoreType.DMA((2,)),
        ],
        compiler_params=_params("arbitrary"),
        name="combine_ln2",
    )(dst_tiles, dst_tiles, x1, mod3, ln_g.reshape(1, d), ln_b.reshape(1, d), yb)


def _route_slots(top_i, gates, n_experts, bm):
    t = top_i.shape[0]
    na = t * TOP_K
    flat_e = top_i.reshape(-1)
    order = jnp.argsort(flat_e).astype(jnp.int32)
    rank = jnp.argsort(order).astype(jnp.int32)
    experts = jnp.arange(n_experts, dtype=jnp.int32)
    onehot = experts[:, None] == flat_e[None, :]
    counts = jnp.sum(onehot.astype(jnp.int32), axis=1)
    starts = jnp.cumsum(counts) - counts
    padded = ((counts + bm - 1) // bm) * bm
    pends = jnp.cumsum(padded)
    pstarts = pends - padded
    shift = jnp.sum(jnp.where(onehot, (pstarts - starts)[:, None], 0), axis=0)
    dest = (shift + rank).reshape(t, TOP_K)
    p = na + n_experts * bm
    nblk = p // bm
    blk_start = jnp.arange(nblk, dtype=jnp.int32) * bm
    blk_e = jnp.minimum(jnp.sum((blk_start[:, None] >= pends[None, :]).astype(jnp.int32), axis=1), n_experts - 1)
    blk_within = blk_start - pstarts[blk_e]
    row = jnp.arange(bm, dtype=jnp.int32)[None, :]
    within = blk_within[:, None] + row
    real = (within < counts[blk_e][:, None]).reshape(p)
    assign = order[jnp.clip(starts[blk_e][:, None] + within, 0, na - 1).reshape(p)]
    slot_tok = jnp.where(real, assign // TOP_K, 0)
    slot_g = jnp.where(real, gates.reshape(-1)[assign], 0.0)
    blk_used = (blk_start < pends[-1]).astype(jnp.int32)
    return slot_tok, slot_g, dest, blk_e, blk_used


def _pad_rope_cols(w):
    half = ROPE_B // 2
    z = jnp.zeros(w.shape[:-1] + (LANES_V7X // 2 - half,), w.dtype)
    return jnp.concatenate([w[..., :half], z, w[..., half:], z], axis=-1)


def _rope_tables(seq):
    half = ROPE_B // 2
    pos = jnp.arange(seq, dtype=F32)
    freqs = ROPE_THETA ** (-jnp.arange(half, dtype=F32) / half)
    ang = pos[:, None] * freqs[None, :]
    cos = jnp.cos(ang)
    sin = jnp.sin(ang)
    z = jnp.zeros((seq, LANES_V7X // 2 - half), F32)
    return (jnp.concatenate([cos, z, cos, z], axis=1), jnp.concatenate([-sin, z, sin, z], axis=1))


def _prepare_layer(w_in, w_uq, w_ukv, w_a_out, w_b_out, w_o, w_gu, w_down):
    d = w_in.shape[0]
    qa_w = N_Q_A * HD_A
    ka_w = N_KV_A * HD_A
    q_lora = w_uq.shape[0]
    kv_lora = w_ukv.shape[0]
    assert q_lora == kv_lora
    group = N_Q_A // N_KV_A
    offs = [0]
    for wdt in (qa_w, ka_w, ka_w, q_lora, kv_lora, ROPE_B, d, d):
        offs.append(offs[-1] + wdt)
    assert offs[-1] == w_in.shape[1]
    part = lambda i: w_in[:, offs[i]:offs[i + 1]]
    w_qa = part(0).reshape(d, N_KV_A, group, HD_A).transpose(0, 2, 1, 3).reshape(d, qa_w) * (HD_A ** -0.5 * LOG2_E)
    cast = lambda w: w.astype(MXU_DTYPE)
    w_uq3 = w_uq.reshape(q_lora, N_H_B, NOPE_B + ROPE_B)
    w_uq_p = jnp.concatenate([w_uq3[..., :NOPE_B], _pad_rope_cols(w_uq3[..., NOPE_B:])], axis=-1)
    w_ukv3 = w_ukv.reshape(kv_lora, N_H_B, NOPE_B + V_B)
    n_exp, _, two_ff = w_gu.shape
    tf_moe = _tile(two_ff // 2, TF_MOE)
    return dict(
        w_qkv=cast(jnp.concatenate([w_qa, part(1), part(2)], axis=1)),
        w_lat=cast(jnp.concatenate([part(3), part(4)], axis=1)),
        w_kr=cast(_pad_rope_cols(part(5))),
        w_gates=cast(jnp.concatenate([part(6), part(7)], axis=1)),
        w_uq=cast(w_uq_p.reshape(q_lora, N_H_B * (NOPE_B + LANES_V7X))),
        w_uk=cast(w_ukv3[..., :NOPE_B].reshape(kv_lora, N_H_B * NOPE_B)),
        w_uv_t=cast(w_ukv3[..., NOPE_B:].reshape(kv_lora, N_H_B * V_B).T),
        w_a_out=cast(w_a_out.reshape(N_KV_A, group, HD_A, d).transpose(1, 0, 2, 3).reshape(qa_w, d)),
        w_b_out=cast(w_b_out),
        w_o=cast(w_o),
        w_gu=cast(w_gu.reshape(n_exp, d, two_ff // tf_moe, tf_moe).transpose(0, 2, 1, 3)),
        w_down=cast(w_down),
    )


def _mixers(x2d, mod3, wts, sinks_a, q_norm_g, kv_norm_g, batch, seq):
    tables = _rope_tables(seq)
    h = _modulate(x2d, mod3, seq)
    qkv = _matmul(h, wts['w_qkv'], MXU_DTYPE, "proj_qkv_a")
    lat = _matmul(h, wts['w_lat'], F32, "proj_latents")
    gates = _matmul(h, wts['w_gates'], MXU_DTYPE, "proj_gates")
    kr = _matmul_rope(h, wts['w_kr'], tables[0], tables[1], seq, "proj_k_rope")

    oa = _window_attention(qkv, sinks_a, batch, seq)

    qscale = (NOPE_B + ROPE_B) ** -0.5 * LOG2_E
    q = _q_up(lat, q_norm_g, wts['w_uq'], tables, seq, qscale)
    kcat = _k_up(lat, kv_norm_g, wts['w_uk'], kr, seq)
    vt = _v_up_t(lat, kv_norm_g, wts['w_uv_t'], _tile(seq, TKC_MLA))
    ob = _mla_attention(q, kcat, vt, batch, seq)
    return _merge(oa, ob, wts['w_a_out'], wts['w_b_out'], gates)


def _encoder_layer(xs, mod, wts, sinks_a, q_norm_g, kv_norm_g, ln1_g, ln1_b, w_router, b_router,
                   b_gu, b_down, ln2_g, ln2_b, alpha):
    d = xs[0].shape[-1]
    batches = [x.shape[0] for x in xs]
    seqs = [x.shape[1] for x in xs]
    rows = [b * s for b, s in zip(batches, seqs)]
    x2d = [x.reshape(r, d) for x, r in zip(xs, rows)]
    mod3 = mod.reshape(mod.shape[0], N_MOD, d)
    merged = []
    b0 = 0
    for x, b, s in zip(x2d, batches, seqs):
        merged.append(_mixers(x, mod3[b0:b0 + b], wts, sinks_a, q_norm_g, kv_norm_g, b, s))
        b0 += b

    x1, h2, ridx, rgate = _oproj_ln1_router(merged, wts['w_o'], x2d, mod3, ln1_g, ln1_b, w_router, b_router,
                                            seqs, alpha)
    n_experts = w_router.shape[1]
    bm = _tile(sum(rows), BM_MOE)
    slot_tok, slot_g, dest, blk_e, blk_used = _route_slots(ridx[:, :TOP_K], rgate[:, :TOP_K], n_experts, bm)
    yb = _moe(h2, blk_e, blk_used, slot_tok, slot_g, wts['w_gu'], b_gu, wts['w_down'], b_down, bm)

    outs = []
    row0 = b0 = 0
    for b, s, r in zip(batches, seqs, rows):
        out = _combine_ln2(dest, x1, mod3, ln2_g, ln2_b, yb, s, alpha, row0, b0, r)
        outs.append(out.reshape(b, s, d))
        row0 += r
        b0 += b
    return outs


def kernel(x_prompt, x_sample, c_prompt, c_sample, w_ada, b_ada, w_in, sinks_a, q_norm_g, kv_norm_g, w_uq, w_ukv, w_a_out, w_b_out, w_o, ln1_g, ln1_b, w_router, b_router, w_gu, b_gu, w_down, b_down, ln2_g, ln2_b):
    depth = w_ada.shape[0]
    alpha = (2.0 * depth) ** 0.25
    pad = (-(c_prompt.shape[0] + c_sample.shape[0])) % SUBLANES_V7X
    c_all = jnp.concatenate([c_prompt, c_sample, jnp.zeros((pad, c_prompt.shape[1]), F32)], axis=0)
    ys = [x_prompt, x_sample]
    for l in range(depth):
        mod = _ada(c_all, w_ada[l], b_ada[l])
        wts = _prepare_layer(w_in[l], w_uq[l], w_ukv[l], w_a_out[l], w_b_out[l], w_o[l], w_gu[l], w_down[l])
        ys = _encoder_layer(ys, mod, wts, sinks_a[l], q_norm_g[l], kv_norm_g[l], ln1_g[l], ln1_b[l],
                            w_router[l], b_router[l], b_gu[l], b_down[l], ln2_g[l], ln2_b[l], alpha)
    return (ys[0], ys[1])
```

```python
import functools

import jax
import jax.numpy as jnp
from jax import lax
from jax.experimental import pallas as pl
from jax.experimental.pallas import tpu as pltpu

N_Q_A = 16
N_KV_A = 2
HD_A = 64
WINDOW = 128
N_H_B = 16
NOPE_B = 128
ROPE_B = 64
V_B = 128
ROPE_THETA = 10000.0
TOP_K = 4
SWIGLU_LIMIT = 7.0
SWIGLU_ALPHA = 1.702
LN_EPS = 1e-5
RMS_EPS = 1e-6
N_MOD = 6
LOG2_E = 1.4426950408889634

LANES_V7X = 128
SUBLANES_V7X = 8
VMEM_LIMIT_BYTES_V7X = 56 * 1024 * 1024

MXU_DTYPE = jnp.bfloat16
F32 = jnp.float32

TM_PROJ = 1024
TN_PROJ = 512
TM_LN = 256
TQ_MLA = 1024
TKC_MLA = 512
BM_MOE = 512
TF_MOE = 1024
TM_FINAL = 128
TN_ADA = 1024


def _tile(n, pref):
    if n <= pref:
        return n
    align = LANES_V7X if pref >= LANES_V7X else SUBLANES_V7X
    t = pref - pref % align
    while t > 0 and n % t:
        t -= align
    assert t > 0, (n, pref)
    return t


def _params(*sem):
    return pltpu.CompilerParams(dimension_semantics=sem, vmem_limit_bytes=VMEM_LIMIT_BYTES_V7X)


def _sigmoid(x):
    return 1.0 / (1.0 + jnp.exp(-x))


def _rope_pairs(x, cos, sin):
    return x * cos + pltpu.roll(x, LANES_V7X // 2, axis=1) * sin


def _ada_kernel(c_ref, w_ref, b_ref, o_ref):
    c = c_ref[...]
    a = (c * _sigmoid(c)).astype(MXU_DTYPE)
    o_ref[...] = jnp.dot(a, w_ref[...].astype(MXU_DTYPE), preferred_element_type=F32) + b_ref[...]


def _ada(c, w_ada, b_ada):
    bp, d = c.shape
    n = w_ada.shape[1]
    tn = _tile(n, TN_ADA)
    return pl.pallas_call(
        _ada_kernel,
        grid=(n // tn,),
        in_specs=[
            pl.BlockSpec((bp, d), lambda j: (0, 0)),
            pl.BlockSpec((d, tn), lambda j: (0, j)),
            pl.BlockSpec((1, tn), lambda j: (0, j)),
        ],
        out_specs=pl.BlockSpec((bp, tn), lambda j: (0, j)),
        out_shape=jax.ShapeDtypeStruct((bp, n), F32),
        compiler_params=_params("parallel"),
        name="ada",
    )(c, w_ada, b_ada.reshape(1, n))


def _mod_kernel(x_ref, mod_ref, o_ref):
    sh = mod_ref[0, 0:1, :]
    sc = mod_ref[0, 1:2, :]
    o_ref[...] = (x_ref[...] * (1.0 + sc) + sh).astype(o_ref.dtype)


def _modulate(x2d, mod3, seq):
    t, d = x2d.shape
    tm = _tile(seq, TM_PROJ)
    per = seq // tm
    return pl.pallas_call(
        _mod_kernel,
        grid=(t // tm,),
        in_specs=[
            pl.BlockSpec((tm, d), lambda i: (i, 0)),
            pl.BlockSpec((1, N_MOD, d), lambda i: (i // per, 0, 0)),
        ],
        out_specs=pl.BlockSpec((tm, d), lambda i: (i, 0)),
        out_shape=jax.ShapeDtypeStruct((t, d), MXU_DTYPE),
        compiler_params=_params("parallel"),
        name="modulate1",
    )(x2d, mod3)


def _mm_kernel(x_ref, w_ref, o_ref):
    o_ref[...] = jnp.dot(x_ref[...], w_ref[...], preferred_element_type=F32).astype(o_ref.dtype)


def _matmul(x, w, out_dtype, name):
    m, k = x.shape
    n = w.shape[1]
    tm = _tile(m, TM_PROJ)
    tn = _tile(n, TN_PROJ)
    return pl.pallas_call(
        _mm_kernel,
        grid=(m // tm, n // tn),
        in_specs=[
            pl.BlockSpec((tm, k), lambda i, j: (i, 0)),
            pl.BlockSpec((k, tn), lambda i, j: (0, j)),
        ],
        out_specs=pl.BlockSpec((tm, tn), lambda i, j: (i, j)),
        out_shape=jax.ShapeDtypeStruct((m, n), out_dtype),
        compiler_params=_params("parallel", "parallel"),
        name=name,
    )(x, w)


def _mm_rope_kernel(x_ref, w_ref, cos_ref, sin_ref, o_ref):
    acc = jnp.dot(x_ref[...], w_ref[...], preferred_element_type=F32)
    o_ref[...] = _rope_pairs(acc, cos_ref[...], sin_ref[...]).astype(o_ref.dtype)


def _matmul_rope(x, w, cos, sin, seq, name):
    m, k = x.shape
    n = w.shape[1]
    assert n == LANES_V7X
    tm = _tile(seq, TM_PROJ)
    per = seq // tm
    return pl.pallas_call(
        _mm_rope_kernel,
        grid=(m // tm,),
        in_specs=[
            pl.BlockSpec((tm, k), lambda i: (i, 0)),
            pl.BlockSpec((k, n), lambda i: (0, 0)),
            pl.BlockSpec((tm, n), lambda i: (i % per, 0)),
            pl.BlockSpec((tm, n), lambda i: (i % per, 0)),
        ],
        out_specs=pl.BlockSpec((tm, n), lambda i: (i, 0)),
        out_shape=jax.ShapeDtypeStruct((m, n), MXU_DTYPE),
        compiler_params=_params("parallel"),
        name=name,
    )(x, w, cos, sin)


def _rms_norm_to(xn_ref, x_ref, g_ref):
    @pl.when(pl.program_id(1) == 0)
    def _():
        xf = x_ref[...]
        r = lax.rsqrt(jnp.mean(xf * xf, axis=-1, keepdims=True) + RMS_EPS)
        xn_ref[...] = ((xf * r) * g_ref[...]).astype(xn_ref.dtype)


def _q_up_kernel(x_ref, g_ref, w_ref, cos_ref, sin_ref, o_ref, xn_ref, *, scale):
    _rms_norm_to(xn_ref, x_ref, g_ref)
    acc = jnp.dot(xn_ref[...], w_ref[...], preferred_element_type=F32) * scale
    cos = cos_ref[...]
    sin = sin_ref[...]
    head_w = NOPE_B + LANES_V7X
    pieces = []
    for c in range(acc.shape[1] // head_w):
        pieces.append(acc[:, c * head_w:c * head_w + NOPE_B])
        pieces.append(_rope_pairs(acc[:, c * head_w + NOPE_B:(c + 1) * head_w], cos, sin))
    o_ref[...] = jnp.concatenate(pieces, axis=1).astype(o_ref.dtype)


def _k_up_kernel(x_ref, g_ref, w_ref, kr_ref, o_ref, xn_ref):
    _rms_norm_to(xn_ref, x_ref, g_ref)
    acc = jnp.dot(xn_ref[...], w_ref[...], preferred_element_type=F32).astype(o_ref.dtype)
    kr = kr_ref[...]
    pieces = []
    for c in range(acc.shape[1] // NOPE_B):
        pieces.append(acc[:, c * NOPE_B:(c + 1) * NOPE_B])
        pieces.append(kr)
    o_ref[...] = jnp.concatenate(pieces, axis=1)


def _v_up_t_kernel(x_ref, g_ref, wt_ref, o_ref, xn_ref):
    _rms_norm_to(xn_ref, x_ref, g_ref)
    acc = lax.dot_general(wt_ref[...], xn_ref[...], (((1,), (1,)), ((), ())), preferred_element_type=F32)
    o_ref[0] = acc.reshape(o_ref.shape[1:]).astype(o_ref.dtype)


def _latent_up(kernel_fn, lat, col_block, gain, w, extra, extra_specs, out_spec, out_shape, tm, grid_n, w_spec, name):
    m = lat.shape[0]
    k = gain.shape[0]
    return pl.pallas_call(
        kernel_fn,
        grid=(m // tm, grid_n),
        in_specs=[
            pl.BlockSpec((tm, k), lambda i, j: (i, col_block)),
            pl.BlockSpec((1, k), lambda i, j: (0, 0)),
            w_spec,
        ] + extra_specs,
        out_specs=out_spec,
        out_shape=out_shape,
        scratch_shapes=[pltpu.VMEM((tm, k), MXU_DTYPE)],
        compiler_params=_params("parallel", "arbitrary"),
        name=name,
    )(lat, gain.reshape(1, k), w, *extra)


def _q_up(lat, gain, w, tables, seq, scale):
    m = lat.shape[0]
    k, n = w.shape
    tm = _tile(seq, TM_PROJ)
    tn = _tile(n, max(TN_PROJ, NOPE_B + LANES_V7X))
    per = seq // tm
    table_spec = pl.BlockSpec((tm, LANES_V7X), lambda i, j: (i % per, 0))
    return _latent_up(
        functools.partial(_q_up_kernel, scale=scale), lat, 0, gain, w, list(tables), [table_spec, table_spec],
        pl.BlockSpec((tm, tn), lambda i, j: (i, j)), jax.ShapeDtypeStruct((m, n), MXU_DTYPE),
        tm, n // tn, pl.BlockSpec((k, tn), lambda i, j: (0, j)), "q_up")


def _k_up(lat, gain, w, kr, seq):
    m = lat.shape[0]
    k, n = w.shape
    tm = _tile(seq, TM_PROJ)
    tn = _tile(n, TN_PROJ)
    return _latent_up(
        _k_up_kernel, lat, 1, gain, w, [kr], [pl.BlockSpec((tm, LANES_V7X), lambda i, j: (i, 0))],
        pl.BlockSpec((tm, 2 * tn), lambda i, j: (i, j)), jax.ShapeDtypeStruct((m, 2 * n), MXU_DTYPE),
        tm, n // tn, pl.BlockSpec((k, tn), lambda i, j: (0, j)), "k_up")


def _v_up_t(lat, gain, wt, chunk):
    m = lat.shape[0]
    n, k = wt.shape
    tn = _tile(n, TN_PROJ)
    heads_per = tn // V_B
    return _latent_up(
        _v_up_t_kernel, lat, 1, gain, wt, [], [],
        pl.BlockSpec((1, heads_per, V_B, chunk), lambda i, j: (i, j, 0, 0)),
        jax.ShapeDtypeStruct((m // chunk, n // V_B, V_B, chunk), MXU_DTYPE),
        chunk, n // tn, pl.BlockSpec((tn, k), lambda i, j: (j, 0)), "v_up_t")


def _window_kernel(q_ref, kp_ref, ko_ref, kn_ref, vp_ref, vo_ref, vn_ref, bias_ref, sink_ref, o_ref):
    w = WINDOW
    group = N_Q_A // N_KV_A
    q = q_ref[...]
    k3 = jnp.concatenate([kp_ref[...], ko_ref[...], kn_ref[...]], axis=0)
    v3 = jnp.concatenate([vp_ref[...], vo_ref[...], vn_ref[...]], axis=0)
    lane = lax.broadcasted_iota(jnp.int32, k3.shape, 1)
    lo = lane < HD_A
    zero = jnp.zeros_like(k3)
    kbd = jnp.concatenate([jnp.where(lo, k3, zero), jnp.where(lo, zero, k3)], axis=0)
    vbd = jnp.concatenate([jnp.where(lo, v3, zero), jnp.where(lo, zero, v3)], axis=0)
    vbd_t = vbd.astype(F32).T.astype(MXU_DTYPE)
    out_lo = lax.broadcasted_iota(jnp.int32, (2 * HD_A, w), 0) < HD_A

    outs = []
    for g in range(group):
        qg = q[:, g * 2 * HD_A:(g + 1) * 2 * HD_A]
        s = lax.dot_general(kbd, qg, (((1,), (1,)), ((), ())), preferred_element_type=F32) + bias_ref[0, g]
        s0 = s[:3 * w]
        s1 = s[3 * w:]
        sink0 = sink_ref[0:1, g:g + 1] * LOG2_E
        sink1 = sink_ref[0:1, group + g:group + g + 1] * LOG2_E
        m0 = jnp.maximum(jnp.max(s0, axis=0, keepdims=True), sink0)
        m1 = jnp.maximum(jnp.max(s1, axis=0, keepdims=True), sink1)
        p0 = jnp.exp2(s0 - m0)
        p1 = jnp.exp2(s1 - m1)
        d0 = jnp.sum(p0, axis=0, keepdims=True) + jnp.exp2(sink0 - m0)
        d1 = jnp.sum(p1, axis=0, keepdims=True) + jnp.exp2(sink1 - m1)
        p = jnp.concatenate([p0, p1], axis=0).astype(MXU_DTYPE)
        o_t = jnp.dot(vbd_t, p, preferred_element_type=F32)
        outs.append(o_t * jnp.where(out_lo, 1.0 / d0, 1.0 / d1))
    o_ref[...] = jnp.concatenate(outs, axis=0).T.astype(o_ref.dtype)


def _window_bias_tables():
    w = WINDOW
    group = N_Q_A // N_KV_A
    key = jnp.arange(6 * w)
    head = key // (3 * w)
    krel = key % (3 * w) - w
    dist = jnp.abs(krel[:, None] - jnp.arange(w)[None, :])
    slopes = jnp.exp2(-8.0 * jnp.arange(1, N_Q_A + 1, dtype=F32) / N_Q_A).reshape(N_KV_A, group)
    slope = slopes[head, :].T
    alibi = -slope[:, :, None] * dist.astype(F32)[None] * LOG2_E
    tables = []
    for case in range(4):
        ok = dist <= w
        if case & 1:
            ok = ok & (krel >= 0)[:, None]
        if case & 2:
            ok = ok & (krel < w)[:, None]
        tables.append(jnp.where(ok[None], alibi, -jnp.inf))
    return jnp.stack(tables)


def _window_attention(qkv, sinks, batch, seq):
    t = qkv.shape[0]
    w = WINDOW
    nb = seq // w
    qw = N_Q_A * HD_A
    kvw = N_KV_A * HD_A
    group = N_Q_A // N_KV_A
    assert N_KV_A == 2 and kvw == LANES_V7X
    kblk = qw // kvw
    vblk = kblk + 1

    def kv_spec(col, off):
        return pl.BlockSpec((w, kvw), lambda b, n: (b * nb + jnp.clip(n + off, 0, nb - 1), col))

    def edge_case(b, n):
        return ((n == 0).astype(jnp.int32) + 2 * (n == nb - 1).astype(jnp.int32), 0, 0, 0)

    return pl.pallas_call(
        _window_kernel,
        grid=(batch, nb),
        in_specs=[
            pl.BlockSpec((w, qw), lambda b, n: (b * nb + n, 0)),
            kv_spec(kblk, -1), kv_spec(kblk, 0), kv_spec(kblk, 1),
            kv_spec(vblk, -1), kv_spec(vblk, 0), kv_spec(vblk, 1),
            pl.BlockSpec((1, group, 6 * w, w), edge_case),
            pl.BlockSpec((1, N_Q_A), lambda b, n: (0, 0)),
        ],
        out_specs=pl.BlockSpec((w, qw), lambda b, n: (b * nb + n, 0)),
        out_shape=jax.ShapeDtypeStruct((t, qw), MXU_DTYPE),
        compiler_params=_params("parallel", "parallel"),
        name="window_attention",
    )(qkv, qkv, qkv, qkv, qkv, qkv, qkv, _window_bias_tables(), sinks.reshape(1, N_Q_A))


def _mla_kernel(q_ref, k_ref, vt_ref, o_ref, s0_ref, s1_ref, m_ref, l_ref, acc_ref, *, nchunk):
    tkc = s0_ref.shape[0]
    q = q_ref[...]
    m_ref[...] = jnp.full(m_ref.shape, -jnp.inf, F32)
    l_ref[...] = jnp.zeros(l_ref.shape, F32)
    acc_ref[...] = jnp.zeros(acc_ref.shape, F32)

    def scores(c, s_ref):
        start = pl.multiple_of(c * tkc, tkc)
        s_ref[...] = lax.dot_general(k_ref[pl.ds(start, tkc), :], q, (((1,), (1,)), ((), ())),
                                     preferred_element_type=F32)

    def accumulate(c, s_ref):
        s = s_ref[...]
        m_old = m_ref[...]
        m_new = jnp.maximum(m_old, jnp.max(s, axis=0, keepdims=True))
        alpha = jnp.exp2(m_old - m_new)
        p = jnp.exp2(s - m_new)
        l_ref[...] = alpha * l_ref[...] + jnp.sum(p, axis=0, keepdims=True)
        acc_ref[...] = alpha * acc_ref[...] + jnp.dot(vt_ref[c], p.astype(MXU_DTYPE), preferred_element_type=F32)
        m_ref[...] = m_new

    scores(0, s0_ref)

    def body(j, carry):
        c = 2 * j
        scores(c + 1, s1_ref)
        accumulate(c, s0_ref)
        scores(c + 2, s0_ref)
        accumulate(c + 1, s1_ref)
        return carry

    lax.fori_loop(0, nchunk // 2 - 1, body, 0)
    scores(nchunk - 1, s1_ref)
    accumulate(nchunk - 2, s0_ref)
    accumulate(nchunk - 1, s1_ref)
    o_ref[...] = (acc_ref[...] * (1.0 / l_ref[...])).T.astype(o_ref.dtype)


def _mla_attention(q, kcat, vt, batch, seq):
    t = q.shape[0]
    tkc = vt.shape[-1]
    nchunk = seq // tkc
    assert nchunk % 2 == 0
    tq = _tile(seq, TQ_MLA)
    nq = seq // tq
    qh = NOPE_B + LANES_V7X
    assert NOPE_B == LANES_V7X and V_B == LANES_V7X
    vt5 = vt.reshape(batch, nchunk, N_H_B, V_B, tkc)
    return pl.pallas_call(
        functools.partial(_mla_kernel, nchunk=nchunk),
        grid=(batch, N_H_B, nq),
        in_specs=[
            pl.BlockSpec((tq, qh), lambda b, h, qi: (b * nq + qi, h)),
            pl.BlockSpec((seq, qh), lambda b, h, qi: (b, h)),
            pl.BlockSpec((None, nchunk, None, V_B, tkc), lambda b, h, qi: (b, 0, h, 0, 0)),
        ],
        out_specs=pl.BlockSpec((tq, V_B), lambda b, h, qi: (b * nq + qi, h)),
        out_shape=jax.ShapeDtypeStruct((t, N_H_B * V_B), MXU_DTYPE),
        scratch_shapes=[
            pltpu.VMEM((tkc, tq), F32),
            pltpu.VMEM((tkc, tq), F32),
            pltpu.VMEM((1, tq), F32),
            pltpu.VMEM((1, tq), F32),
            pltpu.VMEM((V_B, tq), F32),
        ],
        compiler_params=_params("parallel", "parallel", "parallel"),
        name="mla_attention",
    )(q, kcat, vt5)


def _merge_kernel(oa_ref, ob_ref, wa_ref, wb_ref, ga_ref, gb_ref, o_ref):
    a = jnp.dot(oa_ref[...], wa_ref[...], preferred_element_type=F32)
    b = jnp.dot(ob_ref[...], wb_ref[...], preferred_element_type=F32)
    ga = ga_ref[...].astype(F32)
    gb = gb_ref[...].astype(F32)
    o_ref[...] = (_sigmoid(ga) * a + _sigmoid(gb) * b).astype(o_ref.dtype)


def _merge(oa, ob, wa, wb, gates):
    t = oa.shape[0]
    d = wa.shape[1]
    tm = _tile(t, TM_PROJ)
    tn = _tile(d, TN_PROJ)
    nj = d // tn
    return pl.pallas_call(
        _merge_kernel,
        grid=(t // tm, nj),
        in_specs=[
            pl.BlockSpec((tm, oa.shape[1]), lambda i, j: (i, 0)),
            pl.BlockSpec((tm, ob.shape[1]), lambda i, j: (i, 0)),
            pl.BlockSpec((wa.shape[0], tn), lambda i, j: (0, j)),
            pl.BlockSpec((wb.shape[0], tn), lambda i, j: (0, j)),
            pl.BlockSpec((tm, tn), lambda i, j: (i, j)),
            pl.BlockSpec((tm, tn), lambda i, j: (i, nj + j)),
        ],
        out_specs=pl.BlockSpec((tm, tn), lambda i, j: (i, j)),
        out_shape=jax.ShapeDtypeStruct((t, d), MXU_DTYPE),
        compiler_params=_params("parallel", "parallel"),
        name="merge",
    )(oa, ob, wa, wb, gates, gates)


def _layer_norm_rows(z, g, b):
    mu = jnp.mean(z, axis=-1, keepdims=True)
    zc = z - mu
    var = jnp.mean(zc * zc, axis=-1, keepdims=True)
    return (zc * lax.rsqrt(var + LN_EPS)) * g + b


def _ln1_kernel(mga_ref, mgb_ref, wo_ref, xa_ref, xb_ref, mod_ref, lng_ref, lnb_ref, wr_ref, br_ref,
                x1_ref, h2_ref, ri_ref, rg_ref, *, alpha, tiles_a):
    rest = (wo_ref, mod_ref, lng_ref, lnb_ref, wr_ref, br_ref, x1_ref, h2_ref, ri_ref, rg_ref)
    i = pl.program_id(0)

    @pl.when(i < tiles_a)
    def _():
        _ln1_tile(mga_ref, xa_ref, *rest, alpha=alpha)

    @pl.when(i >= tiles_a)
    def _():
        _ln1_tile(mgb_ref, xb_ref, *rest, alpha=alpha)


def _ln1_tile(mg_ref, x_ref, wo_ref, mod_ref, lng_ref, lnb_ref, wr_ref, br_ref,
              x1_ref, h2_ref, ri_ref, rg_ref, *, alpha):
    y = jnp.dot(mg_ref[...], wo_ref[...], preferred_element_type=F32)
    g1 = mod_ref[0, 2:3, :]
    sh2 = mod_ref[0, 3:4, :]
    sc2 = mod_ref[0, 4:5, :]
    x1 = _layer_norm_rows(alpha * x_ref[...] + g1 * y, lng_ref[...], lnb_ref[...])
    x1_ref[...] = x1
    h2 = x1 * (1.0 + sc2) + sh2
    h2_ref[...] = h2
    h_hi = h2.astype(MXU_DTYPE)
    h_lo = (h2 - h_hi.astype(F32)).astype(MXU_DTYPE)
    r_hi = jnp.dot(h_hi, wr_ref[...], preferred_element_type=F32)
    r_lo = jnp.dot(h_lo, wr_ref[:, :LANES_V7X], preferred_element_type=F32)
    logits = r_hi[:, :LANES_V7X] + r_hi[:, LANES_V7X:] + r_lo + br_ref[...]
    ne = logits.shape[1]
    eio = lax.broadcasted_iota(jnp.int32, logits.shape, 1).astype(F32)
    vals, idxs = [], []
    for _ in range(TOP_K):
        mx = jnp.max(logits, axis=1, keepdims=True)
        ix = jnp.min(jnp.where(logits == mx, eio, float(ne)), axis=1, keepdims=True)
        vals.append(mx)
        idxs.append(ix)
        logits = jnp.where(eio == ix, -jnp.inf, logits)
    es = [jnp.exp(v - vals[0]) for v in vals]
    tot = es[0]
    for e in es[1:]:
        tot = tot + e
    lane = lax.broadcasted_iota(jnp.int32, ri_ref.shape, 1)
    ri = jnp.zeros(ri_ref.shape, F32)
    rg = jnp.zeros(rg_ref.shape, F32)
    for k in range(TOP_K):
        ri = jnp.where(lane == k, idxs[k], ri)
        rg = jnp.where(lane == k, es[k] / tot, rg)
    ri_ref[...] = ri.astype(jnp.int32)
    rg_ref[...] = rg


def _oproj_ln1_router(merged_pair, w_o, x_pair, mod3, ln_g, ln_b, w_router, b_router, seqs, alpha):
    (xa, xb), (mga, mgb) = x_pair, merged_pair
    d = xa.shape[1]
    ne = w_router.shape[1]
    assert ne <= LANES_V7X
    tm = _tile(min(seqs), TM_LN)
    assert all(s % tm == 0 for s in seqs)
    per_a, per_b = seqs[0] // tm, seqs[1] // tm
    tiles_a, tiles_b = xa.shape[0] // tm, xb.shape[0] // tm
    batch_a = xa.shape[0] // seqs[0]
    t = xa.shape[0] + xb.shape[0]
    row = lambda i: (i, 0)
    const = lambda i: (0, 0)
    row_a = lambda i: (jnp.minimum(i, tiles_a - 1), 0)
    row_b = lambda i: (jnp.maximum(i - tiles_a, 0), 0)
    mod_row = lambda i: (jnp.where(i < tiles_a, i // per_a, batch_a + (i - tiles_a) // per_b), 0, 0)
    w_hi = w_router.astype(MXU_DTYPE)
    w_lo = (w_router - w_hi.astype(F32)).astype(MXU_DTYPE)
    lane_pad = ((0, 0), (0, LANES_V7X - ne))
    w_split = jnp.concatenate([jnp.pad(w_hi, lane_pad), jnp.pad(w_lo, lane_pad)], axis=1)
    b_pad = jnp.pad(b_router.reshape(1, ne), lane_pad, constant_values=-jnp.inf)
    return pl.pallas_call(
        functools.partial(_ln1_kernel, alpha=alpha, tiles_a=tiles_a),
        grid=(tiles_a + tiles_b,),
        in_specs=[
            pl.BlockSpec((tm, d), row_a),
            pl.BlockSpec((tm, d), row_b),
            pl.BlockSpec((d, d), const),
            pl.BlockSpec((tm, d), row_a),
            pl.BlockSpec((tm, d), row_b),
            pl.BlockSpec((1, N_MOD, d), mod_row),
            pl.BlockSpec((1, d), const),
            pl.BlockSpec((1, d), const),
            pl.BlockSpec((d, 2 * LANES_V7X), const),
            pl.BlockSpec((1, LANES_V7X), const),
        ],
        out_specs=[
            pl.BlockSpec((tm, d), row),
            pl.BlockSpec((tm, d), row),
            pl.BlockSpec((tm, LANES_V7X), row),
            pl.BlockSpec((tm, LANES_V7X), row),
        ],
        out_shape=[
            jax.ShapeDtypeStruct((t, d), F32),
            jax.ShapeDtypeStruct((t, d), F32),
            jax.ShapeDtypeStruct((t, LANES_V7X), jnp.int32),
            jax.ShapeDtypeStruct((t, LANES_V7X), F32),
        ],
        compiler_params=_params("parallel"),
        name="oproj_ln1_router",
    )(mga, mgb, w_o, xa, xb, mod3, ln_g.reshape(1, d), ln_b.reshape(1, d), w_split, b_pad)


def _start_row_gather(src_hbm, idx_ref, dst_ref, sem, dst_slot):
    for r in range(dst_ref.shape[1]):
        pltpu.make_async_copy(src_hbm.at[pl.ds(idx_ref[0, 0, r], 1), :],
                              dst_ref.at[dst_slot, pl.ds(r, 1), :], sem.at[dst_slot]).start()


def _moe_kernel(be_ref, used_ref, tok_ref, tok_next_ref, g_ref, h_hbm, wg_ref, wu_ref, bg_ref, bu_ref, wd_ref, bd_ref,
                o_ref, xg_ref, acc_ref, sem, *, nf):
    del be_ref
    i = pl.program_id(0)
    f = pl.program_id(1)
    nblk = pl.num_programs(0)
    bm = xg_ref.shape[1]
    used = used_ref[i] > 0
    slot = i % 2

    @pl.when(jnp.logical_and(f == 0, jnp.logical_and(i == 0, used)))
    def _():
        _start_row_gather(h_hbm, tok_ref, xg_ref, sem, 0)

    @pl.when(jnp.logical_and(f == 0, used))
    def _():
        pltpu.make_async_copy(h_hbm.at[pl.ds(0, bm), :], xg_ref.at[slot], sem.at[slot]).wait()

    nxt = jnp.minimum(i + 1, nblk - 1)

    prefetch = jnp.logical_and(f == 0, jnp.logical_and(i + 1 < nblk, used_ref[nxt] > 0))
    for next_slot in (0, 1):
        @pl.when(jnp.logical_and(prefetch, slot == 1 - next_slot))
        def _():
            _start_row_gather(h_hbm, tok_next_ref, xg_ref, sem, next_slot)

    def down_contribution():
        x = xg_ref[slot].astype(MXU_DTYPE)
        gate = jnp.dot(x, wg_ref[0, 0], preferred_element_type=F32) + bg_ref[0]
        up = jnp.dot(x, wu_ref[0, 0], preferred_element_type=F32) + bu_ref[0]
        gate = jnp.minimum(gate, SWIGLU_LIMIT)
        up = jnp.clip(up, -SWIGLU_LIMIT, SWIGLU_LIMIT)
        act = (up + 1.0) * gate * _sigmoid(SWIGLU_ALPHA * gate)
        return jnp.dot(act.astype(MXU_DTYPE), wd_ref[0], preferred_element_type=F32)

    first = f == 0
    last = f == nf - 1

    if nf > 1:
        @pl.when(jnp.logical_and(used, first))
        def _():
            acc_ref[...] = down_contribution()

        @pl.when(jnp.logical_and(used, jnp.logical_and(jnp.logical_not(first), jnp.logical_not(last))))
        def _():
            acc_ref[...] += down_contribution()

    @pl.when(jnp.logical_and(used, last))
    def _():
        total = down_contribution()
        if nf > 1:
            total = total + acc_ref[...]
        o_ref[...] = (total + bd_ref[0]) * g_ref[...]

    @pl.when(jnp.logical_and(last, jnp.logical_not(used)))
    def _():
        o_ref[...] = jnp.zeros(o_ref.shape, F32)


def _moe(h2, blk_e, blk_used, slot_tok, slot_g, w_gu, b_gu, w_down, b_down, bm):
    t, d = h2.shape
    ne, two_nf, _, tf = w_gu.shape
    nf = two_nf // 2
    two_ff = two_nf * tf
    p = slot_tok.shape[0]
    nblk = p // bm

    def ftile(i, f, bu):
        return jnp.where(bu[i] > 0, f, nf - 1)

    grid_spec = pltpu.PrefetchScalarGridSpec(
        num_scalar_prefetch=2,
        grid=(nblk, nf),
        in_specs=[
            pl.BlockSpec((1, 1, bm), lambda i, f, be, bu: (i, 0, 0), memory_space=pltpu.SMEM),
            pl.BlockSpec((1, 1, bm), lambda i, f, be, bu: (jnp.minimum(i + 1, nblk - 1), 0, 0),
                         memory_space=pltpu.SMEM),
            pl.BlockSpec((bm, 1), lambda i, f, be, bu: (i, 0)),
            pl.BlockSpec(memory_space=pl.ANY),
            pl.BlockSpec((1, 1, d, tf), lambda i, f, be, bu: (be[i], ftile(i, f, bu), 0, 0)),
            pl.BlockSpec((1, 1, d, tf), lambda i, f, be, bu: (be[i], nf + ftile(i, f, bu), 0, 0)),
            pl.BlockSpec((1, 1, tf), lambda i, f, be, bu: (be[i], 0, ftile(i, f, bu))),
            pl.BlockSpec((1, 1, tf), lambda i, f, be, bu: (be[i], 0, nf + ftile(i, f, bu))),
            pl.BlockSpec((1, tf, d), lambda i, f, be, bu: (be[i], ftile(i, f, bu), 0)),
            pl.BlockSpec((1, 1, d), lambda i, f, be, bu: (be[i], 0, 0)),
        ],
        out_specs=pl.BlockSpec((bm, d), lambda i, f, be, bu: (i, 0)),
        scratch_shapes=[
            pltpu.VMEM((2, bm, d), F32),
            pltpu.VMEM((bm, d), F32),
            pltpu.SemaphoreType.DMA((2,)),
        ],
    )
    slot_tok3 = slot_tok.reshape(nblk, 1, bm)
    return pl.pallas_call(
        functools.partial(_moe_kernel, nf=nf),
        grid_spec=grid_spec,
        out_shape=jax.ShapeDtypeStruct((p, d), F32),
        compiler_params=_params("arbitrary", "arbitrary"),
        name="moe_experts",
    )(blk_e, blk_used, slot_tok3, slot_tok3, slot_g.reshape(p, 1), h2,
      w_gu, w_gu, b_gu.reshape(ne, 1, two_ff), b_gu.reshape(ne, 1, two_ff), w_down, b_down.reshape(ne, 1, d))


def _final_kernel(dst_ref, dst_next_ref, x1_ref, mod_ref, lng_ref, lnb_ref, yb_hbm, o_ref, buf_ref, sem, *, alpha):
    i = pl.program_id(0)
    nt = pl.num_programs(0)
    tm = x1_ref.shape[0]
    nrow = buf_ref.shape[1]
    slot = i % 2

    @pl.when(i == 0)
    def _():
        _start_row_gather(yb_hbm, dst_ref, buf_ref, sem, 0)

    for next_slot in (0, 1):
        @pl.when(jnp.logical_and(i + 1 < nt, slot == 1 - next_slot))
        def _():
            _start_row_gather(yb_hbm, dst_next_ref, buf_ref, sem, next_slot)

    pltpu.make_async_copy(yb_hbm.at[pl.ds(0, nrow), :], buf_ref.at[slot], sem.at[slot]).wait()
    y = buf_ref[slot, 0:tm, :]
    for k in range(1, TOP_K):
        y = y + buf_ref[slot, k * tm:(k + 1) * tm, :]
    g2 = mod_ref[0, 5:6, :]
    o_ref[...] = _layer_norm_rows(alpha * x1_ref[...] + g2 * y, lng_ref[...], lnb_ref[...])


def _combine_ln2(dst, x1, mod3, ln_g, ln_b, yb, seq, alpha, row0, batch0, rows):
    d = x1.shape[1]
    tm = _tile(seq, TM_FINAL)
    per = seq // tm
    nt = rows // tm
    tile0 = row0 // tm
    assert row0 % tm == 0
    nt_all = dst.shape[0] // tm
    dst_tiles = dst.reshape(nt_all, tm, TOP_K).transpose(0, 2, 1).reshape(nt_all, 1, TOP_K * tm)
    return pl.pallas_call(
        functools.partial(_final_kernel, alpha=alpha),
        grid=(nt,),
        in_specs=[
            pl.BlockSpec((1, 1, TOP_K * tm), lambda i: (tile0 + i, 0, 0), memory_space=pltpu.SMEM),
            pl.BlockSpec((1, 1, TOP_K * tm), lambda i: (tile0 + jnp.minimum(i + 1, nt - 1), 0, 0),
                         memory_space=pltpu.SMEM),
            pl.BlockSpec((tm, d), lambda i: (tile0 + i, 0)),
            pl.BlockSpec((1, N_MOD, d), lambda i: (batch0 + i // per, 0, 0)),
            pl.BlockSpec((1, d), lambda i: (0, 0)),
            pl.BlockSpec((1, d), lambda i: (0, 0)),
            pl.BlockSpec(memory_space=pl.ANY),
        ],
        out_specs=pl.BlockSpec((tm, d), lambda i: (i, 0)),
        out_shape=jax.ShapeDtypeStruct((rows, d), F32),
        scratch_shapes=[
            pltpu.VMEM((2, TOP_K * tm, d), F32),
            pltpu.SemaphoreType.DMA((2,)),
        ],
        compiler_params=_params("arbitrary"),
        name="combine_ln2",
    )(dst_tiles, dst_tiles, x1, mod3, ln_g.reshape(1, d), ln_b.reshape(1, d), yb)


def _route_slots(top_i, gates, n_experts, bm):
    t = top_i.shape[0]
    na = t * TOP_K
    flat_e = top_i.reshape(-1)
    order = jnp.argsort(flat_e).astype(jnp.int32)
    rank = jnp.argsort(order).astype(jnp.int32)
    experts = jnp.arange(n_experts, dtype=jnp.int32)
    onehot = experts[:, None] == flat_e[None, :]
    counts = jnp.sum(onehot.astype(jnp.int32), axis=1)
    starts = jnp.cumsum(counts) - counts
    padded = ((counts + bm - 1) // bm) * bm
    pends = jnp.cumsum(padded)
    pstarts = pends - padded
    shift = jnp.sum(jnp.where(onehot, (pstarts - starts)[:, None], 0), axis=0)
    dest = (shift + rank).reshape(t, TOP_K)
    p = na + n_experts * bm
    nblk = p // bm
    blk_start = jnp.arange(nblk, dtype=jnp.int32) * bm
    blk_e = jnp.minimum(jnp.sum((blk_start[:, None] >= pends[None, :]).astype(jnp.int32), axis=1), n_experts - 1)
    blk_within = blk_start - pstarts[blk_e]
    row = jnp.arange(bm, dtype=jnp.int32)[None, :]
    within = blk_within[:, None] + row
    real = (within < counts[blk_e][:, None]).reshape(p)
    assign = order[jnp.clip(starts[blk_e][:, None] + within, 0, na - 1).reshape(p)]
    slot_tok = jnp.where(real, assign // TOP_K, 0)
    slot_g = jnp.where(real, gates.reshape(-1)[assign], 0.0)
    blk_used = (blk_start < pends[-1]).astype(jnp.int32)
    return slot_tok, slot_g, dest, blk_e, blk_used


def _pad_rope_cols(w):
    half = ROPE_B // 2
    z = jnp.zeros(w.shape[:-1] + (LANES_V7X // 2 - half,), w.dtype)
    return jnp.concatenate([w[..., :half], z, w[..., half:], z], axis=-1)


def _rope_tables(seq):
    half = ROPE_B // 2
    pos = jnp.arange(seq, dtype=F32)
    freqs = ROPE_THETA ** (-jnp.arange(half, dtype=F32) / half)
    ang = pos[:, None] * freqs[None, :]
    cos = jnp.cos(ang)
    sin = jnp.sin(ang)
    z = jnp.zeros((seq, LANES_V7X // 2 - half), F32)
    return (jnp.concatenate([cos, z, cos, z], axis=1), jnp.concatenate([-sin, z, sin, z], axis=1))


def _cast_tiles_kernel(w_ref, o_ref):
    o_ref[0, 0] = w_ref[0].astype(o_ref.dtype)


def _cast_gate_up_tiles(w_gu, tf):
    ne, d, two_ff = w_gu.shape
    nt = two_ff // tf
    return pl.pallas_call(
        _cast_tiles_kernel,
        grid=(ne, nt),
        in_specs=[pl.BlockSpec((1, d, tf), lambda e, j: (e, 0, j))],
        out_specs=pl.BlockSpec((1, 1, d, tf), lambda e, j: (e, j, 0, 0)),
        out_shape=jax.ShapeDtypeStruct((ne, nt, d, tf), MXU_DTYPE),
        compiler_params=_params("parallel", "parallel"),
        name="cast_gate_up_tiles",
    )(w_gu)


def _prepare_layer(w_in, w_uq, w_ukv, w_a_out, w_b_out, w_o, w_gu, w_down):
    d = w_in.shape[0]
    qa_w = N_Q_A * HD_A
    ka_w = N_KV_A * HD_A
    q_lora = w_uq.shape[0]
    kv_lora = w_ukv.shape[0]
    assert q_lora == kv_lora
    group = N_Q_A // N_KV_A
    offs = [0]
    for wdt in (qa_w, ka_w, ka_w, q_lora, kv_lora, ROPE_B, d, d):
        offs.append(offs[-1] + wdt)
    assert offs[-1] == w_in.shape[1]
    part = lambda i: w_in[:, offs[i]:offs[i + 1]]
    w_qa = part(0).reshape(d, N_KV_A, group, HD_A).transpose(0, 2, 1, 3).reshape(d, qa_w) * (HD_A ** -0.5 * LOG2_E)
    cast = lambda w: w.astype(MXU_DTYPE)
    w_uq3 = w_uq.reshape(q_lora, N_H_B, NOPE_B + ROPE_B)
    w_uq_p = jnp.concatenate([w_uq3[..., :NOPE_B], _pad_rope_cols(w_uq3[..., NOPE_B:])], axis=-1)
    w_ukv3 = w_ukv.reshape(kv_lora, N_H_B, NOPE_B + V_B)
    tf_moe = _tile(w_gu.shape[2] // 2, TF_MOE)
    return dict(
        w_qkv=cast(jnp.concatenate([w_qa, part(1), part(2)], axis=1)),
        w_lat=cast(jnp.concatenate([part(3), part(4)], axis=1)),
        w_kr=cast(_pad_rope_cols(part(5))),
        w_gates=cast(jnp.concatenate([part(6), part(7)], axis=1)),
        w_uq=cast(w_uq_p.reshape(q_lora, N_H_B * (NOPE_B + LANES_V7X))),
        w_uk=cast(w_ukv3[..., :NOPE_B].reshape(kv_lora, N_H_B * NOPE_B)),
        w_uv_t=cast(w_ukv3[..., NOPE_B:].reshape(kv_lora, N_H_B * V_B).T),
        w_a_out=cast(w_a_out.reshape(N_KV_A, group, HD_A, d).transpose(1, 0, 2, 3).reshape(qa_w, d)),
        w_b_out=cast(w_b_out),
        w_o=cast(w_o),
        w_gu=_cast_gate_up_tiles(w_gu, tf_moe),
        w_down=cast(w_down),
    )


def _mixers(x2d, mod3, wts, sinks_a, q_norm_g, kv_norm_g, batch, seq):
    tables = _rope_tables(seq)
    h = _modulate(x2d, mod3, seq)
    qkv = _matmul(h, wts['w_qkv'], MXU_DTYPE, "proj_qkv_a")
    lat = _matmul(h, wts['w_lat'], F32, "proj_latents")
    gates = _matmul(h, wts['w_gates'], MXU_DTYPE, "proj_gates")
    kr = _matmul_rope(h, wts['w_kr'], tables[0], tables[1], seq, "proj_k_rope")

    oa = _window_attention(qkv, sinks_a, batch, seq)

    qscale = (NOPE_B + ROPE_B) ** -0.5 * LOG2_E
    q = _q_up(lat, q_norm_g, wts['w_uq'], tables, seq, qscale)
    kcat = _k_up(lat, kv_norm_g, wts['w_uk'], kr, seq)
    vt = _v_up_t(lat, kv_norm_g, wts['w_uv_t'], _tile(seq, TKC_MLA))
    ob = _mla_attention(q, kcat, vt, batch, seq)
    return _merge(oa, ob, wts['w_a_out'], wts['w_b_out'], gates)


def _encoder_layer(xs, mod, wts, sinks_a, q_norm_g, kv_norm_g, ln1_g, ln1_b, w_router, b_router,
                   b_gu, b_down, ln2_g, ln2_b, alpha):
    d = xs[0].shape[-1]
    batches = [x.shape[0] for x in xs]
    seqs = [x.shape[1] for x in xs]
    rows = [b * s for b, s in zip(batches, seqs)]
    x2d = [x.reshape(r, d) for x, r in zip(xs, rows)]
    mod3 = mod.reshape(mod.shape[0], N_MOD, d)
    merged = []
    b0 = 0
    for x, b, s in zip(x2d, batches, seqs):
        merged.append(_mixers(x, mod3[b0:b0 + b], wts, sinks_a, q_norm_g, kv_norm_g, b, s))
        b0 += b

    x1, h2, ridx, rgate = _oproj_ln1_router(merged, wts['w_o'], x2d, mod3, ln1_g, ln1_b, w_router, b_router,
                                            seqs, alpha)
    n_experts = w_router.shape[1]
    bm = _tile(sum(rows), BM_MOE)
    slot_tok, slot_g, dest, blk_e, blk_used = _route_slots(ridx[:, :TOP_K], rgate[:, :TOP_K], n_experts, bm)
    yb = _moe(h2, blk_e, blk_used, slot_tok, slot_g, wts['w_gu'], b_gu, wts['w_down'], b_down, bm)

    outs = []
    row0 = b0 = 0
    for b, s, r in zip(batches, seqs, rows):
        out = _combine_ln2(dest, x1, mod3, ln2_g, ln2_b, yb, s, alpha, row0, b0, r)
        outs.append(out.reshape(b, s, d))
        row0 += r
        b0 += b
    return outs


def kernel(x_prompt, x_sample, c_prompt, c_sample, w_ada, b_ada, w_in, sinks_a, q_norm_g, kv_norm_g, w_uq, w_ukv, w_a_out, w_b_out, w_o, ln1_g, ln1_b, w_router, b_router, w_gu, b_gu, w_down, b_down, ln2_g, ln2_b):
    depth = w_ada.shape[0]
    alpha = (2.0 * depth) ** 0.25
    pad = (-(c_prompt.shape[0] + c_sample.shape[0])) % SUBLANES_V7X
    c_all = jnp.concatenate([c_prompt, c_sample, jnp.zeros((pad, c_prompt.shape[1]), F32)], axis=0)
    ys = [x_prompt, x_sample]
    for l in range(depth):
        mod = _ada(c_all, w_ada[l], b_ada[l])
        wts = _prepare_layer(w_in[l], w_uq[l], w_ukv[l], w_a_out[l], w_b_out[l], w_o[l], w_gu[l], w_down[l])
        ys = _encoder_layer(ys, mod, wts, sinks_a[l], q_norm_g[l], kv_norm_g[l], ln1_g[l], ln1_b[l],
                            w_router[l], b_router[l], b_gu[l], b_down[l], ln2_g[l], ln2_b[l], alpha)
    return (ys[0], ys[1])
```

```python
import functools

import jax
import jax.numpy as jnp
from jax import lax
from jax.experimental import pallas as pl
from jax.experimental.pallas import tpu as pltpu

N_Q_A = 16
N_KV_A = 2
HD_A = 64
WINDOW = 128
N_H_B = 16
NOPE_B = 128
ROPE_B = 64
V_B = 128
ROPE_THETA = 10000.0
TOP_K = 4
SWIGLU_LIMIT = 7.0
SWIGLU_ALPHA = 1.702
LN_EPS = 1e-5
RMS_EPS = 1e-6
N_MOD = 6
LOG2_E = 1.4426950408889634

LANES_V7X = 128
SUBLANES_V7X = 8
VMEM_LIMIT_BYTES_V7X = 56 * 1024 * 1024

MXU_DTYPE = jnp.bfloat16
F32 = jnp.float32

TM_PROJ = 1024
TN_PROJ = 512
TM_LN = 256
TQ_MLA = 1024
TKC_MLA = 512
BM_MOE = 512
TF_MOE = 512
TM_FINAL = 128
TN_ADA = 1024


def _tile(n, pref):
    if n <= pref:
        return n
    align = LANES_V7X if pref >= LANES_V7X else SUBLANES_V7X
    t = pref - pref % align
    while t > 0 and n % t:
        t -= align
    assert t > 0, (n, pref)
    return t


def _params(*sem):
    return pltpu.CompilerParams(dimension_semantics=sem, vmem_limit_bytes=VMEM_LIMIT_BYTES_V7X)


def _sigmoid(x):
    return 1.0 / (1.0 + jnp.exp(-x))


def _rope_pairs(x, cos, sin):
    return x * cos + pltpu.roll(x, LANES_V7X // 2, axis=1) * sin


def _ada_kernel(c_ref, w_ref, b_ref, o_ref):
    c = c_ref[...]
    a = (c * _sigmoid(c)).astype(MXU_DTYPE)
    o_ref[...] = jnp.dot(a, w_ref[...].astype(MXU_DTYPE), preferred_element_type=F32) + b_ref[...]


def _ada(c, w_ada, b_ada):
    bp, d = c.shape
    n = w_ada.shape[1]
    tn = _tile(n, TN_ADA)
    return pl.pallas_call(
        _ada_kernel,
        grid=(n // tn,),
        in_specs=[
            pl.BlockSpec((bp, d), lambda j: (0, 0)),
            pl.BlockSpec((d, tn), lambda j: (0, j)),
            pl.BlockSpec((1, tn), lambda j: (0, j)),
        ],
        out_specs=pl.BlockSpec((bp, tn), lambda j: (0, j)),
        out_shape=jax.ShapeDtypeStruct((bp, n), F32),
        compiler_params=_params("parallel"),
        name="ada",
    )(c, w_ada, b_ada.reshape(1, n))


def _mod_kernel(x_ref, mod_ref, o_ref):
    sh = mod_ref[0, 0:1, :]
    sc = mod_ref[0, 1:2, :]
    o_ref[...] = (x_ref[...] * (1.0 + sc) + sh).astype(o_ref.dtype)


def _modulate(x2d, mod3, seq):
    t, d = x2d.shape
    tm = _tile(seq, TM_PROJ)
    per = seq // tm
    return pl.pallas_call(
        _mod_kernel,
        grid=(t // tm,),
        in_specs=[
            pl.BlockSpec((tm, d), lambda i: (i, 0)),
            pl.BlockSpec((1, N_MOD, d), lambda i: (i // per, 0, 0)),
        ],
        out_specs=pl.BlockSpec((tm, d), lambda i: (i, 0)),
        out_shape=jax.ShapeDtypeStruct((t, d), MXU_DTYPE),
        compiler_params=_params("parallel"),
        name="modulate1",
    )(x2d, mod3)


def _mm_kernel(x_ref, w_ref, o_ref):
    o_ref[...] = jnp.dot(x_ref[...], w_ref[...], preferred_element_type=F32).astype(o_ref.dtype)


def _matmul(x, w, out_dtype, name):
    m, k = x.shape
    n = w.shape[1]
    tm = _tile(m, TM_PROJ)
    tn = _tile(n, TN_PROJ)
    return pl.pallas_call(
        _mm_kernel,
        grid=(m // tm, n // tn),
        in_specs=[
            pl.BlockSpec((tm, k), lambda i, j: (i, 0)),
            pl.BlockSpec((k, tn), lambda i, j: (0, j)),
        ],
        out_specs=pl.BlockSpec((tm, tn), lambda i, j: (i, j)),
        out_shape=jax.ShapeDtypeStruct((m, n), out_dtype),
        compiler_params=_params("parallel", "parallel"),
        name=name,
    )(x, w)


def _mm_rope_kernel(x_ref, w_ref, cos_ref, sin_ref, o_ref):
    acc = jnp.dot(x_ref[...], w_ref[...], preferred_element_type=F32)
    o_ref[...] = _rope_pairs(acc, cos_ref[...], sin_ref[...]).astype(o_ref.dtype)


def _matmul_rope(x, w, cos, sin, seq, name):
    m, k = x.shape
    n = w.shape[1]
    assert n == LANES_V7X
    tm = _tile(seq, TM_PROJ)
    per = seq // tm
    return pl.pallas_call(
        _mm_rope_kernel,
        grid=(m // tm,),
        in_specs=[
            pl.BlockSpec((tm, k), lambda i: (i, 0)),
            pl.BlockSpec((k, n), lambda i: (0, 0)),
            pl.BlockSpec((tm, n), lambda i: (i % per, 0)),
            pl.BlockSpec((tm, n), lambda i: (i % per, 0)),
        ],
        out_specs=pl.BlockSpec((tm, n), lambda i: (i, 0)),
        out_shape=jax.ShapeDtypeStruct((m, n), MXU_DTYPE),
        compiler_params=_params("parallel"),
        name=name,
    )(x, w, cos, sin)


def _rms_norm_to(xn_ref, x_ref, g_ref):
    @pl.when(pl.program_id(1) == 0)
    def _():
        xf = x_ref[...]
        r = lax.rsqrt(jnp.mean(xf * xf, axis=-1, keepdims=True) + RMS_EPS)
        xn_ref[...] = ((xf * r) * g_ref[...]).astype(xn_ref.dtype)


def _q_up_kernel(x_ref, g_ref, w_ref, cos_ref, sin_ref, o_ref, xn_ref, *, scale):
    _rms_norm_to(xn_ref, x_ref, g_ref)
    acc = jnp.dot(xn_ref[...], w_ref[...], preferred_element_type=F32) * scale
    cos = cos_ref[...]
    sin = sin_ref[...]
    head_w = NOPE_B + LANES_V7X
    pieces = []
    for c in range(acc.shape[1] // head_w):
        pieces.append(acc[:, c * head_w:c * head_w + NOPE_B])
        pieces.append(_rope_pairs(acc[:, c * head_w + NOPE_B:(c + 1) * head_w], cos, sin))
    o_ref[...] = jnp.concatenate(pieces, axis=1).astype(o_ref.dtype)


def _k_up_kernel(x_ref, g_ref, w_ref, kr_ref, o_ref, xn_ref):
    _rms_norm_to(xn_ref, x_ref, g_ref)
    acc = jnp.dot(xn_ref[...], w_ref[...], preferred_element_type=F32).astype(o_ref.dtype)
    kr = kr_ref[...]
    pieces = []
    for c in range(acc.shape[1] // NOPE_B):
        pieces.append(acc[:, c * NOPE_B:(c + 1) * NOPE_B])
        pieces.append(kr)
    o_ref[...] = jnp.concatenate(pieces, axis=1)


def _v_up_t_kernel(x_ref, g_ref, wt_ref, o_ref, xn_ref):
    _rms_norm_to(xn_ref, x_ref, g_ref)
    acc = lax.dot_general(wt_ref[...], xn_ref[...], (((1,), (1,)), ((), ())), preferred_element_type=F32)
    o_ref[0] = acc.reshape(o_ref.shape[1:]).astype(o_ref.dtype)


def _latent_up(kernel_fn, lat, col_block, gain, w, extra, extra_specs, out_spec, out_shape, tm, grid_n, w_spec, name):
    m = lat.shape[0]
    k = gain.shape[0]
    return pl.pallas_call(
        kernel_fn,
        grid=(m // tm, grid_n),
        in_specs=[
            pl.BlockSpec((tm, k), lambda i, j: (i, col_block)),
            pl.BlockSpec((1, k), lambda i, j: (0, 0)),
            w_spec,
        ] + extra_specs,
        out_specs=out_spec,
        out_shape=out_shape,
        scratch_shapes=[pltpu.VMEM((tm, k), MXU_DTYPE)],
        compiler_params=_params("parallel", "arbitrary"),
        name=name,
    )(lat, gain.reshape(1, k), w, *extra)


def _q_up(lat, gain, w, tables, seq, scale):
    m = lat.shape[0]
    k, n = w.shape
    tm = _tile(seq, TM_PROJ)
    tn = _tile(n, max(TN_PROJ, NOPE_B + LANES_V7X))
    per = seq // tm
    table_spec = pl.BlockSpec((tm, LANES_V7X), lambda i, j: (i % per, 0))
    return _latent_up(
        functools.partial(_q_up_kernel, scale=scale), lat, 0, gain, w, list(tables), [table_spec, table_spec],
        pl.BlockSpec((tm, tn), lambda i, j: (i, j)), jax.ShapeDtypeStruct((m, n), MXU_DTYPE),
        tm, n // tn, pl.BlockSpec((k, tn), lambda i, j: (0, j)), "q_up")


def _k_up(lat, gain, w, kr, seq):
    m = lat.shape[0]
    k, n = w.shape
    tm = _tile(seq, TM_PROJ)
    tn = _tile(n, TN_PROJ)
    return _latent_up(
        _k_up_kernel, lat, 1, gain, w, [kr], [pl.BlockSpec((tm, LANES_V7X), lambda i, j: (i, 0))],
        pl.BlockSpec((tm, 2 * tn), lambda i, j: (i, j)), jax.ShapeDtypeStruct((m, 2 * n), MXU_DTYPE),
        tm, n // tn, pl.BlockSpec((k, tn), lambda i, j: (0, j)), "k_up")


def _v_up_t(lat, gain, wt, chunk):
    m = lat.shape[0]
    n, k = wt.shape
    tn = _tile(n, TN_PROJ)
    heads_per = tn // V_B
    return _latent_up(
        _v_up_t_kernel, lat, 1, gain, wt, [], [],
        pl.BlockSpec((1, heads_per, V_B, chunk), lambda i, j: (i, j, 0, 0)),
        jax.ShapeDtypeStruct((m // chunk, n // V_B, V_B, chunk), MXU_DTYPE),
        chunk, n // tn, pl.BlockSpec((tn, k), lambda i, j: (j, 0)), "v_up_t")


def _window_kernel(q_ref, kp_ref, ko_ref, kn_ref, vp_ref, vo_ref, vn_ref, bias_ref, sink_ref, o_ref):
    w = WINDOW
    group = N_Q_A // N_KV_A
    q = q_ref[...]
    k3 = jnp.concatenate([kp_ref[...], ko_ref[...], kn_ref[...]], axis=0)
    v3 = jnp.concatenate([vp_ref[...], vo_ref[...], vn_ref[...]], axis=0)
    lane = lax.broadcasted_iota(jnp.int32, k3.shape, 1)
    lo = lane < HD_A
    zero = jnp.zeros_like(k3)
    kbd = jnp.concatenate([jnp.where(lo, k3, zero), jnp.where(lo, zero, k3)], axis=0)
    vbd = jnp.concatenate([jnp.where(lo, v3, zero), jnp.where(lo, zero, v3)], axis=0)
    vbd_t = vbd.astype(F32).T.astype(MXU_DTYPE)
    out_lo = lax.broadcasted_iota(jnp.int32, (2 * HD_A, w), 0) < HD_A

    outs = []
    for g in range(group):
        qg = q[:, g * 2 * HD_A:(g + 1) * 2 * HD_A]
        s = lax.dot_general(kbd, qg, (((1,), (1,)), ((), ())), preferred_element_type=F32) + bias_ref[0, g]
        s0 = s[:3 * w]
        s1 = s[3 * w:]
        sink0 = sink_ref[0:1, g:g + 1] * LOG2_E
        sink1 = sink_ref[0:1, group + g:group + g + 1] * LOG2_E
        m0 = jnp.maximum(jnp.max(s0, axis=0, keepdims=True), sink0)
        m1 = jnp.maximum(jnp.max(s1, axis=0, keepdims=True), sink1)
        p0 = jnp.exp2(s0 - m0)
        p1 = jnp.exp2(s1 - m1)
        d0 = jnp.sum(p0, axis=0, keepdims=True) + jnp.exp2(sink0 - m0)
        d1 = jnp.sum(p1, axis=0, keepdims=True) + jnp.exp2(sink1 - m1)
        p = jnp.concatenate([p0, p1], axis=0).astype(MXU_DTYPE)
        o_t = jnp.dot(vbd_t, p, preferred_element_type=F32)
        outs.append(o_t * jnp.where(out_lo, 1.0 / d0, 1.0 / d1))
    o_ref[...] = jnp.concatenate(outs, axis=0).T.astype(o_ref.dtype)


def _window_bias_tables():
    w = WINDOW
    group = N_Q_A // N_KV_A
    key = jnp.arange(6 * w)
    head = key // (3 * w)
    krel = key % (3 * w) - w
    dist = jnp.abs(krel[:, None] - jnp.arange(w)[None, :])
    slopes = jnp.exp2(-8.0 * jnp.arange(1, N_Q_A + 1, dtype=F32) / N_Q_A).reshape(N_KV_A, group)
    slope = slopes[head, :].T
    alibi = -slope[:, :, None] * dist.astype(F32)[None] * LOG2_E
    tables = []
    for case in range(4):
        ok = dist <= w
        if case & 1:
            ok = ok & (krel >= 0)[:, None]
        if case & 2:
            ok = ok & (krel < w)[:, None]
        tables.append(jnp.where(ok[None], alibi, -jnp.inf))
    return jnp.stack(tables)


def _window_attention(qkv, sinks, batch, seq):
    t = qkv.shape[0]
    w = WINDOW
    nb = seq // w
    qw = N_Q_A * HD_A
    kvw = N_KV_A * HD_A
    group = N_Q_A // N_KV_A
    assert N_KV_A == 2 and kvw == LANES_V7X
    kblk = qw // kvw
    vblk = kblk + 1

    def kv_spec(col, off):
        return pl.BlockSpec((w, kvw), lambda b, n: (b * nb + jnp.clip(n + off, 0, nb - 1), col))

    def edge_case(b, n):
        return ((n == 0).astype(jnp.int32) + 2 * (n == nb - 1).astype(jnp.int32), 0, 0, 0)

    return pl.pallas_call(
        _window_kernel,
        grid=(batch, nb),
        in_specs=[
            pl.BlockSpec((w, qw), lambda b, n: (b * nb + n, 0)),
            kv_spec(kblk, -1), kv_spec(kblk, 0), kv_spec(kblk, 1),
            kv_spec(vblk, -1), kv_spec(vblk, 0), kv_spec(vblk, 1),
            pl.BlockSpec((1, group, 6 * w, w), edge_case),
            pl.BlockSpec((1, N_Q_A), lambda b, n: (0, 0)),
        ],
        out_specs=pl.BlockSpec((w, qw), lambda b, n: (b * nb + n, 0)),
        out_shape=jax.ShapeDtypeStruct((t, qw), MXU_DTYPE),
        compiler_params=_params("parallel", "parallel"),
        name="window_attention",
    )(qkv, qkv, qkv, qkv, qkv, qkv, qkv, _window_bias_tables(), sinks.reshape(1, N_Q_A))


def _mla_kernel(q_ref, k_ref, vt_ref, o_ref, s0_ref, s1_ref, m_ref, l_ref, acc_ref, *, nchunk):
    tkc = s0_ref.shape[0]
    q = q_ref[...]
    m_ref[...] = jnp.full(m_ref.shape, -jnp.inf, F32)
    l_ref[...] = jnp.zeros(l_ref.shape, F32)
    acc_ref[...] = jnp.zeros(acc_ref.shape, F32)

    def scores(c, s_ref):
        start = pl.multiple_of(c * tkc, tkc)
        s_ref[...] = lax.dot_general(k_ref[pl.ds(start, tkc), :], q, (((1,), (1,)), ((), ())),
                                     preferred_element_type=F32)

    def accumulate(c, s_ref):
        s = s_ref[...]
        m_old = m_ref[...]
        m_new = jnp.maximum(m_old, jnp.max(s, axis=0, keepdims=True))
        alpha = jnp.exp2(m_old - m_new)
        p = jnp.exp2(s - m_new)
        l_ref[...] = alpha * l_ref[...] + jnp.sum(p, axis=0, keepdims=True)
        acc_ref[...] = alpha * acc_ref[...] + jnp.dot(vt_ref[c], p.astype(MXU_DTYPE), preferred_element_type=F32)
        m_ref[...] = m_new

    scores(0, s0_ref)

    def body(j, carry):
        c = 2 * j
        scores(c + 1, s1_ref)
        accumulate(c, s0_ref)
        scores(c + 2, s0_ref)
        accumulate(c + 1, s1_ref)
        return carry

    lax.fori_loop(0, nchunk // 2 - 1, body, 0)
    scores(nchunk - 1, s1_ref)
    accumulate(nchunk - 2, s0_ref)
    accumulate(nchunk - 1, s1_ref)
    o_ref[...] = (acc_ref[...] * (1.0 / l_ref[...])).T.astype(o_ref.dtype)


def _mla_attention(q, kcat, vt, batch, seq):
    t = q.shape[0]
    tkc = vt.shape[-1]
    nchunk = seq // tkc
    assert nchunk % 2 == 0
    tq = _tile(seq, TQ_MLA)
    nq = seq // tq
    qh = NOPE_B + LANES_V7X
    assert NOPE_B == LANES_V7X and V_B == LANES_V7X
    vt5 = vt.reshape(batch, nchunk, N_H_B, V_B, tkc)
    return pl.pallas_call(
        functools.partial(_mla_kernel, nchunk=nchunk),
        grid=(batch, N_H_B, nq),
        in_specs=[
            pl.BlockSpec((tq, qh), lambda b, h, qi: (b * nq + qi, h)),
            pl.BlockSpec((seq, qh), lambda b, h, qi: (b, h)),
            pl.BlockSpec((None, nchunk, None, V_B, tkc), lambda b, h, qi: (b, 0, h, 0, 0)),
        ],
        out_specs=pl.BlockSpec((tq, V_B), lambda b, h, qi: (b * nq + qi, h)),
        out_shape=jax.ShapeDtypeStruct((t, N_H_B * V_B), MXU_DTYPE),
        scratch_shapes=[
            pltpu.VMEM((tkc, tq), F32),
            pltpu.VMEM((tkc, tq), F32),
            pltpu.VMEM((1, tq), F32),
            pltpu.VMEM((1, tq), F32),
            pltpu.VMEM((V_B, tq), F32),
        ],
        compiler_params=_params("parallel", "parallel", "parallel"),
        name="mla_attention",
    )(q, kcat, vt5)


def _merge_kernel(oa_ref, ob_ref, wa_ref, wb_ref, ga_ref, gb_ref, o_ref):
    a = jnp.dot(oa_ref[...], wa_ref[...], preferred_element_type=F32)
    b = jnp.dot(ob_ref[...], wb_ref[...], preferred_element_type=F32)
    ga = ga_ref[...].astype(F32)
    gb = gb_ref[...].astype(F32)
    o_ref[...] = (_sigmoid(ga) * a + _sigmoid(gb) * b).astype(o_ref.dtype)


def _merge(oa, ob, wa, wb, gates):
    t = oa.shape[0]
    d = wa.shape[1]
    tm = _tile(t, TM_PROJ)
    tn = _tile(d, TN_PROJ)
    nj = d // tn
    return pl.pallas_call(
        _merge_kernel,
        grid=(t // tm, nj),
        in_specs=[
            pl.BlockSpec((tm, oa.shape[1]), lambda i, j: (i, 0)),
            pl.BlockSpec((tm, ob.shape[1]), lambda i, j: (i, 0)),
            pl.BlockSpec((wa.shape[0], tn), lambda i, j: (0, j)),
            pl.BlockSpec((wb.shape[0], tn), lambda i, j: (0, j)),
            pl.BlockSpec((tm, tn), lambda i, j: (i, j)),
            pl.BlockSpec((tm, tn), lambda i, j: (i, nj + j)),
        ],
        out_specs=pl.BlockSpec((tm, tn), lambda i, j: (i, j)),
        out_shape=jax.ShapeDtypeStruct((t, d), MXU_DTYPE),
        compiler_params=_params("parallel", "parallel"),
        name="merge",
    )(oa, ob, wa, wb, gates, gates)


def _layer_norm_rows(z, g, b):
    mu = jnp.mean(z, axis=-1, keepdims=True)
    zc = z - mu
    var = jnp.mean(zc * zc, axis=-1, keepdims=True)
    return (zc * lax.rsqrt(var + LN_EPS)) * g + b


def _ln1_kernel(mga_ref, mgb_ref, wo_ref, xa_ref, xb_ref, mod_ref, lng_ref, lnb_ref, wr_ref, br_ref,
                x1_ref, h2_ref, ri_ref, rg_ref, *, alpha, tiles_a):
    rest = (wo_ref, mod_ref, lng_ref, lnb_ref, wr_ref, br_ref, x1_ref, h2_ref, ri_ref, rg_ref)
    i = pl.program_id(0)

    @pl.when(i < tiles_a)
    def _():
        _ln1_tile(mga_ref, xa_ref, *rest, alpha=alpha)

    @pl.when(i >= tiles_a)
    def _():
        _ln1_tile(mgb_ref, xb_ref, *rest, alpha=alpha)


def _ln1_tile(mg_ref, x_ref, wo_ref, mod_ref, lng_ref, lnb_ref, wr_ref, br_ref,
              x1_ref, h2_ref, ri_ref, rg_ref, *, alpha):
    y = jnp.dot(mg_ref[...], wo_ref[...], preferred_element_type=F32)
    g1 = mod_ref[0, 2:3, :]
    sh2 = mod_ref[0, 3:4, :]
    sc2 = mod_ref[0, 4:5, :]
    x1 = _layer_norm_rows(alpha * x_ref[...] + g1 * y, lng_ref[...], lnb_ref[...])
    x1_ref[...] = x1
    h2 = x1 * (1.0 + sc2) + sh2
    h2_ref[...] = h2
    h_hi = h2.astype(MXU_DTYPE)
    h_lo = (h2 - h_hi.astype(F32)).astype(MXU_DTYPE)
    r_hi = jnp.dot(h_hi, wr_ref[...], preferred_element_type=F32)
    r_lo = jnp.dot(h_lo, wr_ref[:, :LANES_V7X], preferred_element_type=F32)
    logits = r_hi[:, :LANES_V7X] + r_hi[:, LANES_V7X:] + r_lo + br_ref[...]
    ne = logits.shape[1]
    eio = lax.broadcasted_iota(jnp.int32, logits.shape, 1).astype(F32)
    vals, idxs = [], []
    for _ in range(TOP_K):
        mx = jnp.max(logits, axis=1, keepdims=True)
        ix = jnp.min(jnp.where(logits == mx, eio, float(ne)), axis=1, keepdims=True)
        vals.append(mx)
        idxs.append(ix)
        logits = jnp.where(eio == ix, -jnp.inf, logits)
    es = [jnp.exp(v - vals[0]) for v in vals]
    tot = es[0]
    for e in es[1:]:
        tot = tot + e
    lane = lax.broadcasted_iota(jnp.int32, ri_ref.shape, 1)
    ri = jnp.zeros(ri_ref.shape, F32)
    rg = jnp.zeros(rg_ref.shape, F32)
    for k in range(TOP_K):
        ri = jnp.where(lane == k, idxs[k], ri)
        rg = jnp.where(lane == k, es[k] / tot, rg)
    ri_ref[...] = ri.astype(jnp.int32)
    rg_ref[...] = rg


def _oproj_ln1_router(merged_pair, w_o, x_pair, mod3, ln_g, ln_b, w_router, b_router, seqs, alpha):
    (xa, xb), (mga, mgb) = x_pair, merged_pair
    d = xa.shape[1]
    ne = w_router.shape[1]
    assert ne <= LANES_V7X
    tm = _tile(min(seqs), TM_LN)
    assert all(s % tm == 0 for s in seqs)
    per_a, per_b = seqs[0] // tm, seqs[1] // tm
    tiles_a, tiles_b = xa.shape[0] // tm, xb.shape[0] // tm
    batch_a = xa.shape[0] // seqs[0]
    t = xa.shape[0] + xb.shape[0]
    row = lambda i: (i, 0)
    const = lambda i: (0, 0)
    row_a = lambda i: (jnp.minimum(i, tiles_a - 1), 0)
    row_b = lambda i: (jnp.maximum(i - tiles_a, 0), 0)
    mod_row = lambda i: (jnp.where(i < tiles_a, i // per_a, batch_a + (i - tiles_a) // per_b), 0, 0)
    w_hi = w_router.astype(MXU_DTYPE)
    w_lo = (w_router - w_hi.astype(F32)).astype(MXU_DTYPE)
    lane_pad = ((0, 0), (0, LANES_V7X - ne))
    w_split = jnp.concatenate([jnp.pad(w_hi, lane_pad), jnp.pad(w_lo, lane_pad)], axis=1)
    b_pad = jnp.pad(b_router.reshape(1, ne), lane_pad, constant_values=-jnp.inf)
    return pl.pallas_call(
        functools.partial(_ln1_kernel, alpha=alpha, tiles_a=tiles_a),
        grid=(tiles_a + tiles_b,),
        in_specs=[
            pl.BlockSpec((tm, d), row_a),
            pl.BlockSpec((tm, d), row_b),
            pl.BlockSpec((d, d), const),
            pl.BlockSpec((tm, d), row_a),
            pl.BlockSpec((tm, d), row_b),
            pl.BlockSpec((1, N_MOD, d), mod_row),
            pl.BlockSpec((1, d), const),
            pl.BlockSpec((1, d), const),
            pl.BlockSpec((d, 2 * LANES_V7X), const),
            pl.BlockSpec((1, LANES_V7X), const),
        ],
        out_specs=[
            pl.BlockSpec((tm, d), row),
            pl.BlockSpec((tm, d), row),
            pl.BlockSpec((tm, LANES_V7X), row),
            pl.BlockSpec((tm, LANES_V7X), row),
        ],
        out_shape=[
            jax.ShapeDtypeStruct((t, d), F32),
            jax.ShapeDtypeStruct((t, d), F32),
            jax.ShapeDtypeStruct((t, LANES_V7X), jnp.int32),
            jax.ShapeDtypeStruct((t, LANES_V7X), F32),
        ],
        compiler_params=_params("parallel"),
        name="oproj_ln1_router",
    )(mga, mgb, w_o, xa, xb, mod3, ln_g.reshape(1, d), ln_b.reshape(1, d), w_split, b_pad)


def _start_row_gather(src_hbm, idx_ref, dst_ref, sem, dst_slot):
    for r in range(dst_ref.shape[1]):
        pltpu.make_async_copy(src_hbm.at[pl.ds(idx_ref[0, 0, r], 1), :],
                              dst_ref.at[dst_slot, pl.ds(r, 1), :], sem.at[dst_slot]).start()


def _moe_kernel(be_ref, used_ref, tok_ref, tok_next_ref, g_ref, h_hbm, wg_ref, wu_ref, bg_ref, bu_ref, wd_ref, bd_ref,
                o_ref, xg_ref, acc_ref, sem, *, nf):
    del be_ref
    i = pl.program_id(0)
    f = pl.program_id(1)
    nblk = pl.num_programs(0)
    bm = xg_ref.shape[1]
    used = used_ref[i] > 0
    slot = i % 2

    @pl.when(jnp.logical_and(f == 0, jnp.logical_and(i == 0, used)))
    def _():
        _start_row_gather(h_hbm, tok_ref, xg_ref, sem, 0)

    @pl.when(jnp.logical_and(f == 0, used))
    def _():
        pltpu.make_async_copy(h_hbm.at[pl.ds(0, bm), :], xg_ref.at[slot], sem.at[slot]).wait()

    nxt = jnp.minimum(i + 1, nblk - 1)

    prefetch = jnp.logical_and(f == 0, jnp.logical_and(i + 1 < nblk, used_ref[nxt] > 0))
    for next_slot in (0, 1):
        @pl.when(jnp.logical_and(prefetch, slot == 1 - next_slot))
        def _():
            _start_row_gather(h_hbm, tok_next_ref, xg_ref, sem, next_slot)

    def down_contribution():
        x = xg_ref[slot].astype(MXU_DTYPE)
        gate = jnp.dot(x, wg_ref[0, 0], preferred_element_type=F32) + bg_ref[0]
        up = jnp.dot(x, wu_ref[0, 0], preferred_element_type=F32) + bu_ref[0]
        gate = jnp.minimum(gate, SWIGLU_LIMIT)
        up = jnp.clip(up, -SWIGLU_LIMIT, SWIGLU_LIMIT)
        act = (up + 1.0) * gate * _sigmoid(SWIGLU_ALPHA * gate)
        return jnp.dot(act.astype(MXU_DTYPE), wd_ref[0], preferred_element_type=F32)

    first = f == 0
    last = f == nf - 1

    if nf > 1:
        @pl.when(jnp.logical_and(used, first))
        def _():
            acc_ref[...] = down_contribution()

        @pl.when(jnp.logical_and(used, jnp.logical_and(jnp.logical_not(first), jnp.logical_not(last))))
        def _():
            acc_ref[...] += down_contribution()

    @pl.when(jnp.logical_and(used, last))
    def _():
        total = down_contribution()
        if nf > 1:
            total = total + acc_ref[...]
        o_ref[...] = (total + bd_ref[0]) * g_ref[...]

    @pl.when(jnp.logical_and(last, jnp.logical_not(used)))
    def _():
        o_ref[...] = jnp.zeros(o_ref.shape, F32)


def _moe(h2, blk_e, blk_used, slot_tok, slot_g, w_gu, b_gu, w_down, b_down, bm):
    t, d = h2.shape
    ne, two_nf, _, tf = w_gu.shape
    nf = two_nf // 2
    two_ff = two_nf * tf
    p = slot_tok.shape[0]
    nblk = p // bm

    def ftile(i, f, bu):
        return jnp.where(bu[i] > 0, f, nf - 1)

    grid_spec = pltpu.PrefetchScalarGridSpec(
        num_scalar_prefetch=2,
        grid=(nblk, nf),
        in_specs=[
            pl.BlockSpec((1, 1, bm), lambda i, f, be, bu: (i, 0, 0), memory_space=pltpu.SMEM),
            pl.BlockSpec((1, 1, bm), lambda i, f, be, bu: (jnp.minimum(i + 1, nblk - 1), 0, 0),
                         memory_space=pltpu.SMEM),
            pl.BlockSpec((bm, 1), lambda i, f, be, bu: (i, 0)),
            pl.BlockSpec(memory_space=pl.ANY),
            pl.BlockSpec((1, 1, d, tf), lambda i, f, be, bu: (be[i], ftile(i, f, bu), 0, 0)),
            pl.BlockSpec((1, 1, d, tf), lambda i, f, be, bu: (be[i], nf + ftile(i, f, bu), 0, 0)),
            pl.BlockSpec((1, 1, tf), lambda i, f, be, bu: (be[i], 0, ftile(i, f, bu))),
            pl.BlockSpec((1, 1, tf), lambda i, f, be, bu: (be[i], 0, nf + ftile(i, f, bu))),
            pl.BlockSpec((1, tf, d), lambda i, f, be, bu: (be[i], ftile(i, f, bu), 0)),
            pl.BlockSpec((1, 1, d), lambda i, f, be, bu: (be[i], 0, 0)),
        ],
        out_specs=pl.BlockSpec((bm, d), lambda i, f, be, bu: (i, 0)),
        scratch_shapes=[
            pltpu.VMEM((2, bm, d), F32),
            pltpu.VMEM((bm, d), F32),
            pltpu.SemaphoreType.DMA((2,)),
        ],
    )
    slot_tok3 = slot_tok.reshape(nblk, 1, bm)
    return pl.pallas_call(
        functools.partial(_moe_kernel, nf=nf),
        grid_spec=grid_spec,
        out_shape=jax.ShapeDtypeStruct((p, d), F32),
        compiler_params=_params("arbitrary", "arbitrary"),
        name="moe_experts",
    )(blk_e, blk_used, slot_tok3, slot_tok3, slot_g.reshape(p, 1), h2,
      w_gu, w_gu, b_gu.reshape(ne, 1, two_ff), b_gu.reshape(ne, 1, two_ff), w_down, b_down.reshape(ne, 1, d))


def _final_kernel(dst_ref, dst_next_ref, x1_ref, mod_ref, lng_ref, lnb_ref, yb_hbm, o_ref, buf_ref, sem, *, alpha):
    i = pl.program_id(0)
    nt = pl.num_programs(0)
    tm = x1_ref.shape[0]
    nrow = buf_ref.shape[1]
    slot = i % 2

    @pl.when(i == 0)
    def _():
        _start_row_gather(yb_hbm, dst_ref, buf_ref, sem, 0)

    for next_slot in (0, 1):
        @pl.when(jnp.logical_and(i + 1 < nt, slot == 1 - next_slot))
        def _():
            _start_row_gather(yb_hbm, dst_next_ref, buf_ref, sem, next_slot)

    pltpu.make_async_copy(yb_hbm.at[pl.ds(0, nrow), :], buf_ref.at[slot], sem.at[slot]).wait()
    y = buf_ref[slot, 0:tm, :]
    for k in range(1, TOP_K):
        y = y + buf_ref[slot, k * tm:(k + 1) * tm, :]
    g2 = mod_ref[0, 5:6, :]
    o_ref[...] = _layer_norm_rows(alpha * x1_ref[...] + g2 * y, lng_ref[...], lnb_ref[...])


def _combine_ln2(dst, x1, mod3, ln_g, ln_b, yb, seq, alpha, row0, batch0, rows):
    d = x1.shape[1]
    tm = _tile(seq, TM_FINAL)
    per = seq // tm
    nt = rows // tm
    tile0 = row0 // tm
    assert row0 % tm == 0
    nt_all = dst.shape[0] // tm
    dst_tiles = dst.reshape(nt_all, tm, TOP_K).transpose(0, 2, 1).reshape(nt_all, 1, TOP_K * tm)
    return pl.pallas_call(
        functools.partial(_final_kernel, alpha=alpha),
        grid=(nt,),
        in_specs=[
            pl.BlockSpec((1, 1, TOP_K * tm), lambda i: (tile0 + i, 0, 0), memory_space=pltpu.SMEM),
            pl.BlockSpec((1, 1, TOP_K * tm), lambda i: (tile0 + jnp.minimum(i + 1, nt - 1), 0, 0),
                         memory_space=pltpu.SMEM),
            pl.BlockSpec((tm, d), lambda i: (tile0 + i, 0)),
            pl.BlockSpec((1, N_MOD, d), lambda i: (batch0 + i // per, 0, 0)),
            pl.BlockSpec((1, d), lambda i: (0, 0)),
            pl.BlockSpec((1, d), lambda i: (0, 0)),
            pl.BlockSpec(memory_space=pl.ANY),
        ],
        out_specs=pl.BlockSpec((tm, d), lambda i: (i, 0)),
        out_shape=jax.ShapeDtypeStruct((rows, d), F32),
        scratch_shapes=[
            pltpu.VMEM((2, TOP_K * tm, d), F32),
            pltpu.SemaphoreType.DMA((2,)),
        ],
        compiler_params=_params("arbitrary"),
        name="combine_ln2",
    )(dst_tiles, dst_tiles, x1, mod3, ln_g.reshape(1, d), ln_b.reshape(1, d), yb)


def _route_slots(top_i, gates, n_experts, bm):
    t = top_i.shape[0]
    na = t * TOP_K
    flat_e = top_i.reshape(-1)
    order = jnp.argsort(flat_e).astype(jnp.int32)
    rank = jnp.argsort(order).astype(jnp.int32)
    experts = jnp.arange(n_experts, dtype=jnp.int32)
    onehot = experts[:, None] == flat_e[None, :]
    counts = jnp.sum(onehot.astype(jnp.int32), axis=1)
    starts = jnp.cumsum(counts) - counts
    padded = ((counts + bm - 1) // bm) * bm
    pends = jnp.cumsum(padded)
    pstarts = pends - padded
    shift = jnp.sum(jnp.where(onehot, (pstarts - starts)[:, None], 0), axis=0)
    dest = (shift + rank).reshape(t, TOP_K)
    p = na + n_experts * bm
    nblk = p // bm
    blk_start = jnp.arange(nblk, dtype=jnp.int32) * bm
    blk_e = jnp.minimum(jnp.sum((blk_start[:, None] >= pends[None, :]).astype(jnp.int32), axis=1), n_experts - 1)
    blk_within = blk_start - pstarts[blk_e]
    row = jnp.arange(bm, dtype=jnp.int32)[None, :]
    within = blk_within[:, None] + row
    real = (within < counts[blk_e][:, None]).reshape(p)
    assign = order[jnp.clip(starts[blk_e][:, None] + within, 0, na - 1).reshape(p)]
    slot_tok = jnp.where(real, assign // TOP_K, 0)
    slot_g = jnp.where(real, gates.reshape(-1)[assign], 0.0)
    blk_used = (blk_start < pends[-1]).astype(jnp.int32)
    return slot_tok, slot_g, dest, blk_e, blk_used


def _pad_rope_cols(w):
    half = ROPE_B // 2
    z = jnp.zeros(w.shape[:-1] + (LANES_V7X // 2 - half,), w.dtype)
    return jnp.concatenate([w[..., :half], z, w[..., half:], z], axis=-1)


def _rope_tables(seq):
    half = ROPE_B // 2
    pos = jnp.arange(seq, dtype=F32)
    freqs = ROPE_THETA ** (-jnp.arange(half, dtype=F32) / half)
    ang = pos[:, None] * freqs[None, :]
    cos = jnp.cos(ang)
    sin = jnp.sin(ang)
    z = jnp.zeros((seq, LANES_V7X // 2 - half), F32)
    return (jnp.concatenate([cos, z, cos, z], axis=1), jnp.concatenate([-sin, z, sin, z], axis=1))


def _cast_tiles_kernel(w_ref, o_ref):
    o_ref[0, 0] = w_ref[0].astype(o_ref.dtype)


def _cast_gate_up_tiles(w_gu, tf):
    ne, d, two_ff = w_gu.shape
    nt = two_ff // tf
    return pl.pallas_call(
        _cast_tiles_kernel,
        grid=(ne, nt),
        in_specs=[pl.BlockSpec((1, d, tf), lambda e, j: (e, 0, j))],
        out_specs=pl.BlockSpec((1, 1, d, tf), lambda e, j: (e, j, 0, 0)),
        out_shape=jax.ShapeDtypeStruct((ne, nt, d, tf), MXU_DTYPE),
        compiler_params=_params("parallel", "parallel"),
        name="cast_gate_up_tiles",
    )(w_gu)


def _prepare_layer(w_in, w_uq, w_ukv, w_a_out, w_b_out, w_o, w_gu, w_down):
    d = w_in.shape[0]
    qa_w = N_Q_A * HD_A
    ka_w = N_KV_A * HD_A
    q_lora = w_uq.shape[0]
    kv_lora = w_ukv.shape[0]
    assert q_lora == kv_lora
    group = N_Q_A // N_KV_A
    offs = [0]
    for wdt in (qa_w, ka_w, ka_w, q_lora, kv_lora, ROPE_B, d, d):
        offs.append(offs[-1] + wdt)
    assert offs[-1] == w_in.shape[1]
    part = lambda i: w_in[:, offs[i]:offs[i + 1]]
    w_qa = part(0).reshape(d, N_KV_A, group, HD_A).transpose(0, 2, 1, 3).reshape(d, qa_w) * (HD_A ** -0.5 * LOG2_E)
    cast = lambda w: w.astype(MXU_DTYPE)
    w_uq3 = w_uq.reshape(q_lora, N_H_B, NOPE_B + ROPE_B)
    w_uq_p = jnp.concatenate([w_uq3[..., :NOPE_B], _pad_rope_cols(w_uq3[..., NOPE_B:])], axis=-1)
    w_ukv3 = w_ukv.reshape(kv_lora, N_H_B, NOPE_B + V_B)
    tf_moe = _tile(w_gu.shape[2] // 2, TF_MOE)
    return dict(
        w_qkv=cast(jnp.concatenate([w_qa, part(1), part(2)], axis=1)),
        w_lat=cast(jnp.concatenate([part(3), part(4)], axis=1)),
        w_kr=cast(_pad_rope_cols(part(5))),
        w_gates=cast(jnp.concatenate([part(6), part(7)], axis=1)),
        w_uq=cast(w_uq_p.reshape(q_lora, N_H_B * (NOPE_B + LANES_V7X))),
        w_uk=cast(w_ukv3[..., :NOPE_B].reshape(kv_lora, N_H_B * NOPE_B)),
        w_uv_t=cast(w_ukv3[..., NOPE_B:].reshape(kv_lora, N_H_B * V_B).T),
        w_a_out=cast(w_a_out.reshape(N_KV_A, group, HD_A, d).transpose(1, 0, 2, 3).reshape(qa_w, d)),
        w_b_out=cast(w_b_out),
        w_o=cast(w_o),
        w_gu=_cast_gate_up_tiles(w_gu, tf_moe),
        w_down=cast(w_down),
    )


def _mixers(x2d, mod3, wts, sinks_a, q_norm_g, kv_norm_g, batch, seq):
    tables = _rope_tables(seq)
    h = _modulate(x2d, mod3, seq)
    qkv = _matmul(h, wts['w_qkv'], MXU_DTYPE, "proj_qkv_a")
    lat = _matmul(h, wts['w_lat'], F32, "proj_latents")
    gates = _matmul(h, wts['w_gates'], MXU_DTYPE, "proj_gates")
    kr = _matmul_rope(h, wts['w_kr'], tables[0], tables[1], seq, "proj_k_rope")

    oa = _window_attention(qkv, sinks_a, batch, seq)

    qscale = (NOPE_B + ROPE_B) ** -0.5 * LOG2_E
    q = _q_up(lat, q_norm_g, wts['w_uq'], tables, seq, qscale)
    kcat = _k_up(lat, kv_norm_g, wts['w_uk'], kr, seq)
    vt = _v_up_t(lat, kv_norm_g, wts['w_uv_t'], _tile(seq, TKC_MLA))
    ob = _mla_attention(q, kcat, vt, batch, seq)
    return _merge(oa, ob, wts['w_a_out'], wts['w_b_out'], gates)


def _encoder_layer(xs, mod, wts, sinks_a, q_norm_g, kv_norm_g, ln1_g, ln1_b, w_router, b_router,
                   b_gu, b_down, ln2_g, ln2_b, alpha):
    d = xs[0].shape[-1]
    batches = [x.shape[0] for x in xs]
    seqs = [x.shape[1] for x in xs]
    rows = [b * s for b, s in zip(batches, seqs)]
    x2d = [x.reshape(r, d) for x, r in zip(xs, rows)]
    mod3 = mod.reshape(mod.shape[0], N_MOD, d)
    merged = []
    b0 = 0
    for x, b, s in zip(x2d, batches, seqs):
        merged.append(_mixers(x, mod3[b0:b0 + b], wts, sinks_a, q_norm_g, kv_norm_g, b, s))
        b0 += b

    x1, h2, ridx, rgate = _oproj_ln1_router(merged, wts['w_o'], x2d, mod3, ln1_g, ln1_b, w_router, b_router,
                                            seqs, alpha)
    n_experts = w_router.shape[1]
    bm = _tile(sum(rows), BM_MOE)
    slot_tok, slot_g, dest, blk_e, blk_used = _route_slots(ridx[:, :TOP_K], rgate[:, :TOP_K], n_experts, bm)
    yb = _moe(h2, blk_e, blk_used, slot_tok, slot_g, wts['w_gu'], b_gu, wts['w_down'], b_down, bm)

    outs = []
    row0 = b0 = 0
    for b, s, r in zip(batches, seqs, rows):
        out = _combine_ln2(dest, x1, mod3, ln2_g, ln2_b, yb, s, alpha, row0, b0, r)
        outs.append(out.reshape(b, s, d))
        row0 += r
        b0 += b
    return outs


def kernel(x_prompt, x_sample, c_prompt, c_sample, w_ada, b_ada, w_in, sinks_a, q_norm_g, kv_norm_g, w_uq, w_ukv, w_a_out, w_b_out, w_o, ln1_g, ln1_b, w_router, b_router, w_gu, b_gu, w_down, b_down, ln2_g, ln2_b):
    depth = w_ada.shape[0]
    alpha = (2.0 * depth) ** 0.25
    pad = (-(c_prompt.shape[0] + c_sample.shape[0])) % SUBLANES_V7X
    c_all = jnp.concatenate([c_prompt, c_sample, jnp.zeros((pad, c_prompt.shape[1]), F32)], axis=0)
    ys = [x_prompt, x_sample]
    for l in range(depth):
        mod = _ada(c_all, w_ada[l], b_ada[l])
        wts = _prepare_layer(w_in[l], w_uq[l], w_ukv[l], w_a_out[l], w_b_out[l], w_o[l], w_gu[l], w_down[l])
        ys = _encoder_layer(ys, mod, wts, sinks_a[l], q_norm_g[l], kv_norm_g[l], ln1_g[l], ln1_b[l],
                            w_router[l], b_router[l], b_gu[l], b_down[l], ln2_g[l], ln2_b[l], alpha)
    return (ys[0], ys[1])
```

```python
import functools

import jax
import jax.numpy as jnp
from jax import lax
from jax.experimental import pallas as pl
from jax.experimental.pallas import tpu as pltpu

N_Q_A = 16
N_KV_A = 2
HD_A = 64
WINDOW = 128
N_H_B = 16
NOPE_B = 128
ROPE_B = 64
V_B = 128
ROPE_THETA = 10000.0
TOP_K = 4
SWIGLU_LIMIT = 7.0
SWIGLU_ALPHA = 1.702
LN_EPS = 1e-5
RMS_EPS = 1e-6
N_MOD = 6
LOG2_E = 1.4426950408889634
ONES_ROWS = 16

LANES_V7X = 128
SUBLANES_V7X = 8
VMEM_LIMIT_BYTES_V7X = 56 * 1024 * 1024

MXU_DTYPE = jnp.bfloat16
F32 = jnp.float32

TM_PROJ = 1024
TN_PROJ = 512
TM_LN = 256
TQ_MLA = 1024
TKC_MLA = 512
BM_MOE = 512
TF_MOE = 512
TM_FINAL = 128
TN_ADA = 1024


def _tile(n, pref):
    if n <= pref:
        return n
    align = LANES_V7X if pref >= LANES_V7X else SUBLANES_V7X
    t = pref - pref % align
    while t > 0 and n % t:
        t -= align
    assert t > 0, (n, pref)
    return t


def _params(*sem):
    return pltpu.CompilerParams(dimension_semantics=sem, vmem_limit_bytes=VMEM_LIMIT_BYTES_V7X)


def _sigmoid(x):
    return 1.0 / (1.0 + jnp.exp(-x))


def _rope_pairs(x, cos, sin):
    return x * cos + pltpu.roll(x, LANES_V7X // 2, axis=1) * sin


def _ada_kernel(c_ref, w_ref, b_ref, o_ref):
    c = c_ref[...]
    a = (c * _sigmoid(c)).astype(MXU_DTYPE)
    o_ref[...] = jnp.dot(a, w_ref[...].astype(MXU_DTYPE), preferred_element_type=F32) + b_ref[...]


def _ada(c, w_ada, b_ada):
    bp, d = c.shape
    n = w_ada.shape[1]
    tn = _tile(n, TN_ADA)
    return pl.pallas_call(
        _ada_kernel,
        grid=(n // tn,),
        in_specs=[
            pl.BlockSpec((bp, d), lambda j: (0, 0)),
            pl.BlockSpec((d, tn), lambda j: (0, j)),
            pl.BlockSpec((1, tn), lambda j: (0, j)),
        ],
        out_specs=pl.BlockSpec((bp, tn), lambda j: (0, j)),
        out_shape=jax.ShapeDtypeStruct((bp, n), F32),
        compiler_params=_params("parallel"),
        name="ada",
    )(c, w_ada, b_ada.reshape(1, n))


def _mod_kernel(x_ref, mod_ref, o_ref):
    sh = mod_ref[0, 0:1, :]
    sc = mod_ref[0, 1:2, :]
    o_ref[...] = (x_ref[...] * (1.0 + sc) + sh).astype(o_ref.dtype)


def _modulate(x2d, mod3, seq):
    t, d = x2d.shape
    tm = _tile(seq, TM_PROJ)
    per = seq // tm
    return pl.pallas_call(
        _mod_kernel,
        grid=(t // tm,),
        in_specs=[
            pl.BlockSpec((tm, d), lambda i: (i, 0)),
            pl.BlockSpec((1, N_MOD, d), lambda i: (i // per, 0, 0)),
        ],
        out_specs=pl.BlockSpec((tm, d), lambda i: (i, 0)),
        out_shape=jax.ShapeDtypeStruct((t, d), MXU_DTYPE),
        compiler_params=_params("parallel"),
        name="modulate1",
    )(x2d, mod3)


def _mm_kernel(x_ref, w_ref, o_ref):
    o_ref[...] = jnp.dot(x_ref[...], w_ref[...], preferred_element_type=F32).astype(o_ref.dtype)


def _matmul(x, w, out_dtype, name):
    m, k = x.shape
    n = w.shape[1]
    tm = _tile(m, TM_PROJ)
    tn = _tile(n, TN_PROJ)
    return pl.pallas_call(
        _mm_kernel,
        grid=(m // tm, n // tn),
        in_specs=[
            pl.BlockSpec((tm, k), lambda i, j: (i, 0)),
            pl.BlockSpec((k, tn), lambda i, j: (0, j)),
        ],
        out_specs=pl.BlockSpec((tm, tn), lambda i, j: (i, j)),
        out_shape=jax.ShapeDtypeStruct((m, n), out_dtype),
        compiler_params=_params("parallel", "parallel"),
        name=name,
    )(x, w)


def _mm_rope_kernel(x_ref, w_ref, cos_ref, sin_ref, o_ref):
    acc = jnp.dot(x_ref[...], w_ref[...], preferred_element_type=F32)
    o_ref[...] = _rope_pairs(acc, cos_ref[...], sin_ref[...]).astype(o_ref.dtype)


def _matmul_rope(x, w, cos, sin, seq, name):
    m, k = x.shape
    n = w.shape[1]
    assert n == LANES_V7X
    tm = _tile(seq, TM_PROJ)
    per = seq // tm
    return pl.pallas_call(
        _mm_rope_kernel,
        grid=(m // tm,),
        in_specs=[
            pl.BlockSpec((tm, k), lambda i: (i, 0)),
            pl.BlockSpec((k, n), lambda i: (0, 0)),
            pl.BlockSpec((tm, n), lambda i: (i % per, 0)),
            pl.BlockSpec((tm, n), lambda i: (i % per, 0)),
        ],
        out_specs=pl.BlockSpec((tm, n), lambda i: (i, 0)),
        out_shape=jax.ShapeDtypeStruct((m, n), MXU_DTYPE),
        compiler_params=_params("parallel"),
        name=name,
    )(x, w, cos, sin)


def _rms_norm_to(xn_ref, x_ref, g_ref):
    @pl.when(pl.program_id(1) == 0)
    def _():
        xf = x_ref[...]
        r = lax.rsqrt(jnp.mean(xf * xf, axis=-1, keepdims=True) + RMS_EPS)
        xn_ref[...] = ((xf * r) * g_ref[...]).astype(xn_ref.dtype)


def _q_up_kernel(x_ref, g_ref, w_ref, cos_ref, sin_ref, o_ref, xn_ref, *, scale):
    _rms_norm_to(xn_ref, x_ref, g_ref)
    acc = jnp.dot(xn_ref[...], w_ref[...], preferred_element_type=F32) * scale
    cos = cos_ref[...]
    sin = sin_ref[...]
    head_w = NOPE_B + LANES_V7X
    pieces = []
    for c in range(acc.shape[1] // head_w):
        pieces.append(acc[:, c * head_w:c * head_w + NOPE_B])
        pieces.append(_rope_pairs(acc[:, c * head_w + NOPE_B:(c + 1) * head_w], cos, sin))
    o_ref[...] = jnp.concatenate(pieces, axis=1).astype(o_ref.dtype)


def _k_up_kernel(x_ref, g_ref, w_ref, kr_ref, o_ref, xn_ref):
    _rms_norm_to(xn_ref, x_ref, g_ref)
    acc = jnp.dot(xn_ref[...], w_ref[...], preferred_element_type=F32).astype(o_ref.dtype)
    kr = kr_ref[...]
    pieces = []
    for c in range(acc.shape[1] // NOPE_B):
        pieces.append(acc[:, c * NOPE_B:(c + 1) * NOPE_B])
        pieces.append(kr)
    o_ref[...] = jnp.concatenate(pieces, axis=1)


def _v_up_t_kernel(x_ref, g_ref, wt_ref, o_ref, xn_ref):
    _rms_norm_to(xn_ref, x_ref, g_ref)
    acc = lax.dot_general(wt_ref[...], xn_ref[...], (((1,), (1,)), ((), ())), preferred_element_type=F32)
    o_ref[0] = acc.reshape(o_ref.shape[1:]).astype(o_ref.dtype)


def _latent_up(kernel_fn, lat, col_block, gain, w, extra, extra_specs, out_spec, out_shape, tm, grid_n, w_spec, name):
    m = lat.shape[0]
    k = gain.shape[0]
    return pl.pallas_call(
        kernel_fn,
        grid=(m // tm, grid_n),
        in_specs=[
            pl.BlockSpec((tm, k), lambda i, j: (i, col_block)),
            pl.BlockSpec((1, k), lambda i, j: (0, 0)),
            w_spec,
        ] + extra_specs,
        out_specs=out_spec,
        out_shape=out_shape,
        scratch_shapes=[pltpu.VMEM((tm, k), MXU_DTYPE)],
        compiler_params=_params("parallel", "arbitrary"),
        name=name,
    )(lat, gain.reshape(1, k), w, *extra)


def _q_up(lat, gain, w, tables, seq, scale):
    m = lat.shape[0]
    k, n = w.shape
    tm = _tile(seq, TM_PROJ)
    tn = _tile(n, max(TN_PROJ, NOPE_B + LANES_V7X))
    per = seq // tm
    table_spec = pl.BlockSpec((tm, LANES_V7X), lambda i, j: (i % per, 0))
    return _latent_up(
        functools.partial(_q_up_kernel, scale=scale), lat, 0, gain, w, list(tables), [table_spec, table_spec],
        pl.BlockSpec((tm, tn), lambda i, j: (i, j)), jax.ShapeDtypeStruct((m, n), MXU_DTYPE),
        tm, n // tn, pl.BlockSpec((k, tn), lambda i, j: (0, j)), "q_up")


def _k_up(lat, gain, w, kr, seq):
    m = lat.shape[0]
    k, n = w.shape
    tm = _tile(seq, TM_PROJ)
    tn = _tile(n, TN_PROJ)
    return _latent_up(
        _k_up_kernel, lat, 1, gain, w, [kr], [pl.BlockSpec((tm, LANES_V7X), lambda i, j: (i, 0))],
        pl.BlockSpec((tm, 2 * tn), lambda i, j: (i, j)), jax.ShapeDtypeStruct((m, 2 * n), MXU_DTYPE),
        tm, n // tn, pl.BlockSpec((k, tn), lambda i, j: (0, j)), "k_up")


def _v_up_t(lat, gain, wt, chunk):
    m = lat.shape[0]
    n, k = wt.shape
    tn = _tile(n, TN_PROJ)
    heads_per = tn // V_B
    return _latent_up(
        _v_up_t_kernel, lat, 1, gain, wt, [], [],
        pl.BlockSpec((1, heads_per, V_B, chunk), lambda i, j: (i, j, 0, 0)),
        jax.ShapeDtypeStruct((m // chunk, n // V_B, V_B, chunk), MXU_DTYPE),
        chunk, n // tn, pl.BlockSpec((tn, k), lambda i, j: (j, 0)), "v_up_t")


def _window_kernel(q_ref, kp_ref, ko_ref, kn_ref, vp_ref, vo_ref, vn_ref, bias_ref, sink_ref, o_ref):
    w = WINDOW
    group = N_Q_A // N_KV_A
    q = q_ref[...]
    k3 = jnp.concatenate([kp_ref[...], ko_ref[...], kn_ref[...]], axis=0)
    v3 = jnp.concatenate([vp_ref[...], vo_ref[...], vn_ref[...]], axis=0)
    lane = lax.broadcasted_iota(jnp.int32, k3.shape, 1)
    lo = lane < HD_A
    zero = jnp.zeros_like(k3)
    kbd = jnp.concatenate([jnp.where(lo, k3, zero), jnp.where(lo, zero, k3)], axis=0)
    vbd = jnp.concatenate([jnp.where(lo, v3, zero), jnp.where(lo, zero, v3)], axis=0)
    vbd_t = vbd.astype(F32).T.astype(MXU_DTYPE)
    out_lo = lax.broadcasted_iota(jnp.int32, (2 * HD_A, w), 0) < HD_A

    outs = []
    for g in range(group):
        qg = q[:, g * 2 * HD_A:(g + 1) * 2 * HD_A]
        s = lax.dot_general(kbd, qg, (((1,), (1,)), ((), ())), preferred_element_type=F32) + bias_ref[0, g]
        s0 = s[:3 * w]
        s1 = s[3 * w:]
        sink0 = sink_ref[0:1, g:g + 1] * LOG2_E
        sink1 = sink_ref[0:1, group + g:group + g + 1] * LOG2_E
        m0 = jnp.maximum(jnp.max(s0, axis=0, keepdims=True), sink0)
        m1 = jnp.maximum(jnp.max(s1, axis=0, keepdims=True), sink1)
        p0 = jnp.exp2(s0 - m0)
        p1 = jnp.exp2(s1 - m1)
        d0 = jnp.sum(p0, axis=0, keepdims=True) + jnp.exp2(sink0 - m0)
        d1 = jnp.sum(p1, axis=0, keepdims=True) + jnp.exp2(sink1 - m1)
        p = jnp.concatenate([p0, p1], axis=0).astype(MXU_DTYPE)
        o_t = jnp.dot(vbd_t, p, preferred_element_type=F32)
        outs.append(o_t * jnp.where(out_lo, 1.0 / d0, 1.0 / d1))
    o_ref[...] = jnp.concatenate(outs, axis=0).T.astype(o_ref.dtype)


def _window_bias_tables():
    w = WINDOW
    group = N_Q_A // N_KV_A
    key = jnp.arange(6 * w)
    head = key // (3 * w)
    krel = key % (3 * w) - w
    dist = jnp.abs(krel[:, None] - jnp.arange(w)[None, :])
    slopes = jnp.exp2(-8.0 * jnp.arange(1, N_Q_A + 1, dtype=F32) / N_Q_A).reshape(N_KV_A, group)
    slope = slopes[head, :].T
    alibi = -slope[:, :, None] * dist.astype(F32)[None] * LOG2_E
    tables = []
    for case in range(4):
        ok = dist <= w
        if case & 1:
            ok = ok & (krel >= 0)[:, None]
        if case & 2:
            ok = ok & (krel < w)[:, None]
        tables.append(jnp.where(ok[None], alibi, -jnp.inf))
    return jnp.stack(tables)


def _window_attention(qkv, sinks, batch, seq):
    t = qkv.shape[0]
    w = WINDOW
    nb = seq // w
    qw = N_Q_A * HD_A
    kvw = N_KV_A * HD_A
    group = N_Q_A // N_KV_A
    assert N_KV_A == 2 and kvw == LANES_V7X
    kblk = qw // kvw
    vblk = kblk + 1

    def kv_spec(col, off):
        return pl.BlockSpec((w, kvw), lambda b, n: (b * nb + jnp.clip(n + off, 0, nb - 1), col))

    def edge_case(b, n):
        return ((n == 0).astype(jnp.int32) + 2 * (n == nb - 1).astype(jnp.int32), 0, 0, 0)

    return pl.pallas_call(
        _window_kernel,
        grid=(batch, nb),
        in_specs=[
            pl.BlockSpec((w, qw), lambda b, n: (b * nb + n, 0)),
            kv_spec(kblk, -1), kv_spec(kblk, 0), kv_spec(kblk, 1),
            kv_spec(vblk, -1), kv_spec(vblk, 0), kv_spec(vblk, 1),
            pl.BlockSpec((1, group, 6 * w, w), edge_case),
            pl.BlockSpec((1, N_Q_A), lambda b, n: (0, 0)),
        ],
        out_specs=pl.BlockSpec((w, qw), lambda b, n: (b * nb + n, 0)),
        out_shape=jax.ShapeDtypeStruct((t, qw), MXU_DTYPE),
        compiler_params=_params("parallel", "parallel"),
        name="window_attention",
    )(qkv, qkv, qkv, qkv, qkv, qkv, qkv, _window_bias_tables(), sinks.reshape(1, N_Q_A))


def _mla_kernel(q_ref, k_ref, vt_ref, o_ref, s0_ref, s1_ref, m_ref, acc_ref, *, nchunk):
    tkc = s0_ref.shape[0]
    q = q_ref[...]
    m_ref[...] = jnp.full(m_ref.shape, -jnp.inf, F32)
    acc_ref[...] = jnp.zeros(acc_ref.shape, F32)

    def scores(c, s_ref):
        start = pl.multiple_of(c * tkc, tkc)
        s_ref[...] = lax.dot_general(k_ref[pl.ds(start, tkc), :], q, (((1,), (1,)), ((), ())),
                                     preferred_element_type=F32)

    def accumulate(c, s_ref):
        s = s_ref[...]
        m_old = m_ref[...]
        m_new = jnp.maximum(m_old, jnp.max(s, axis=0, keepdims=True))
        alpha = jnp.exp2(m_old - m_new)
        p = jnp.exp2(s - m_new)
        vt_ones = jnp.concatenate([vt_ref[c], jnp.ones((ONES_ROWS, tkc), MXU_DTYPE)], axis=0)
        acc_ref[...] = alpha * acc_ref[...] + jnp.dot(vt_ones, p.astype(MXU_DTYPE), preferred_element_type=F32)
        m_ref[...] = m_new

    scores(0, s0_ref)

    def body(j, carry):
        c = 2 * j
        scores(c + 1, s1_ref)
        accumulate(c, s0_ref)
        scores(c + 2, s0_ref)
        accumulate(c + 1, s1_ref)
        return carry

    lax.fori_loop(0, nchunk // 2 - 1, body, 0)
    scores(nchunk - 1, s1_ref)
    accumulate(nchunk - 2, s0_ref)
    accumulate(nchunk - 1, s1_ref)
    o_ref[...] = (acc_ref[0:V_B, :] * (1.0 / acc_ref[V_B:V_B + 1, :])).T.astype(o_ref.dtype)


def _mla_attention(q, kcat, vt, batch, seq):
    t = q.shape[0]
    tkc = vt.shape[-1]
    nchunk = seq // tkc
    assert nchunk % 2 == 0
    tq = _tile(seq, TQ_MLA)
    nq = seq // tq
    qh = NOPE_B + LANES_V7X
    assert NOPE_B == LANES_V7X and V_B == LANES_V7X
    vt5 = vt.reshape(batch, nchunk, N_H_B, V_B, tkc)
    return pl.pallas_call(
        functools.partial(_mla_kernel, nchunk=nchunk),
        grid=(batch, N_H_B, nq),
        in_specs=[
            pl.BlockSpec((tq, qh), lambda b, h, qi: (b * nq + qi, h)),
            pl.BlockSpec((seq, qh), lambda b, h, qi: (b, h)),
            pl.BlockSpec((None, nchunk, None, V_B, tkc), lambda b, h, qi: (b, 0, h, 0, 0)),
        ],
        out_specs=pl.BlockSpec((tq, V_B), lambda b, h, qi: (b * nq + qi, h)),
        out_shape=jax.ShapeDtypeStruct((t, N_H_B * V_B), MXU_DTYPE),
        scratch_shapes=[
            pltpu.VMEM((tkc, tq), F32),
            pltpu.VMEM((tkc, tq), F32),
            pltpu.VMEM((1, tq), F32),
            pltpu.VMEM((V_B + ONES_ROWS, tq), F32),
        ],
        compiler_params=_params("parallel", "parallel", "parallel"),
        name="mla_attention",
    )(q, kcat, vt5)


def _merge_kernel(oa_ref, ob_ref, wa_ref, wb_ref, ga_ref, gb_ref, o_ref):
    a = jnp.dot(oa_ref[...], wa_ref[...], preferred_element_type=F32)
    b = jnp.dot(ob_ref[...], wb_ref[...], preferred_element_type=F32)
    ga = ga_ref[...].astype(F32)
    gb = gb_ref[...].astype(F32)
    o_ref[...] = (_sigmoid(ga) * a + _sigmoid(gb) * b).astype(o_ref.dtype)


def _merge(oa, ob, wa, wb, gates):
    t = oa.shape[0]
    d = wa.shape[1]
    tm = _tile(t, TM_PROJ)
    tn = _tile(d, TN_PROJ)
    nj = d // tn
    return pl.pallas_call(
        _merge_kernel,
        grid=(t // tm, nj),
        in_specs=[
            pl.BlockSpec((tm, oa.shape[1]), lambda i, j: (i, 0)),
            pl.BlockSpec((tm, ob.shape[1]), lambda i, j: (i, 0)),
            pl.BlockSpec((wa.shape[0], tn), lambda i, j: (0, j)),
            pl.BlockSpec((wb.shape[0], tn), lambda i, j: (0, j)),
            pl.BlockSpec((tm, tn), lambda i, j: (i, j)),
            pl.BlockSpec((tm, tn), lambda i, j: (i, nj + j)),
        ],
        out_specs=pl.BlockSpec((tm, tn), lambda i, j: (i, j)),
        out_shape=jax.ShapeDtypeStruct((t, d), MXU_DTYPE),
        compiler_params=_params("parallel", "parallel"),
        name="merge",
    )(oa, ob, wa, wb, gates, gates)


def _layer_norm_rows(z, g, b):
    mu = jnp.mean(z, axis=-1, keepdims=True)
    zc = z - mu
    var = jnp.mean(zc * zc, axis=-1, keepdims=True)
    return (zc * lax.rsqrt(var + LN_EPS)) * g + b


def _ln1_kernel(mga_ref, mgb_ref, wo_ref, xa_ref, xb_ref, mod_ref, lng_ref, lnb_ref, wr_ref, br_ref,
                x1_ref, h2_ref, ri_ref, rg_ref, *, alpha, tiles_a):
    rest = (wo_ref, mod_ref, lng_ref, lnb_ref, wr_ref, br_ref, x1_ref, h2_ref, ri_ref, rg_ref)
    i = pl.program_id(0)

    @pl.when(i < tiles_a)
    def _():
        _ln1_tile(mga_ref, xa_ref, *rest, alpha=alpha)

    @pl.when(i >= tiles_a)
    def _():
        _ln1_tile(mgb_ref, xb_ref, *rest, alpha=alpha)


def _ln1_tile(mg_ref, x_ref, wo_ref, mod_ref, lng_ref, lnb_ref, wr_ref, br_ref,
              x1_ref, h2_ref, ri_ref, rg_ref, *, alpha):
    y = jnp.dot(mg_ref[...], wo_ref[...], preferred_element_type=F32)
    g1 = mod_ref[0, 2:3, :]
    sh2 = mod_ref[0, 3:4, :]
    sc2 = mod_ref[0, 4:5, :]
    x1 = _layer_norm_rows(alpha * x_ref[...] + g1 * y, lng_ref[...], lnb_ref[...])
    x1_ref[...] = x1
    h2 = x1 * (1.0 + sc2) + sh2
    h2_ref[...] = h2
    h_hi = h2.astype(MXU_DTYPE)
    h_lo = (h2 - h_hi.astype(F32)).astype(MXU_DTYPE)
    r_hi = jnp.dot(h_hi, wr_ref[...], preferred_element_type=F32)
    r_lo = jnp.dot(h_lo, wr_ref[:, :LANES_V7X], preferred_element_type=F32)
    logits = r_hi[:, :LANES_V7X] + r_hi[:, LANES_V7X:] + r_lo + br_ref[...]
    ne = logits.shape[1]
    eio = lax.broadcasted_iota(jnp.int32, logits.shape, 1).astype(F32)
    vals, idxs = [], []
    for _ in range(TOP_K):
        mx = jnp.max(logits, axis=1, keepdims=True)
        ix = jnp.min(jnp.where(logits == mx, eio, float(ne)), axis=1, keepdims=True)
        vals.append(mx)
        idxs.append(ix)
        logits = jnp.where(eio == ix, -jnp.inf, logits)
    es = [jnp.exp(v - vals[0]) for v in vals]
    tot = es[0]
    for e in es[1:]:
        tot = tot + e
    lane = lax.broadcasted_iota(jnp.int32, ri_ref.shape, 1)
    ri = jnp.zeros(ri_ref.shape, F32)
    rg = jnp.zeros(rg_ref.shape, F32)
    for k in range(TOP_K):
        ri = jnp.where(lane == k, idxs[k], ri)
        rg = jnp.where(lane == k, es[k] / tot, rg)
    ri_ref[...] = ri.astype(jnp.int32)
    rg_ref[...] = rg


def _oproj_ln1_router(merged_pair, w_o, x_pair, mod3, ln_g, ln_b, w_router, b_router, seqs, alpha):
    (xa, xb), (mga, mgb) = x_pair, merged_pair
    d = xa.shape[1]
    ne = w_router.shape[1]
    assert ne <= LANES_V7X
    tm = _tile(min(seqs), TM_LN)
    assert all(s % tm == 0 for s in seqs)
    per_a, per_b = seqs[0] // tm, seqs[1] // tm
    tiles_a, tiles_b = xa.shape[0] // tm, xb.shape[0] // tm
    batch_a = xa.shape[0] // seqs[0]
    t = xa.shape[0] + xb.shape[0]
    row = lambda i: (i, 0)
    const = lambda i: (0, 0)
    row_a = lambda i: (jnp.minimum(i, tiles_a - 1), 0)
    row_b = lambda i: (jnp.maximum(i - tiles_a, 0), 0)
    mod_row = lambda i: (jnp.where(i < tiles_a, i // per_a, batch_a + (i - tiles_a) // per_b), 0, 0)
    w_hi = w_router.astype(MXU_DTYPE)
    w_lo = (w_router - w_hi.astype(F32)).astype(MXU_DTYPE)
    lane_pad = ((0, 0), (0, LANES_V7X - ne))
    w_split = jnp.concatenate([jnp.pad(w_hi, lane_pad), jnp.pad(w_lo, lane_pad)], axis=1)
    b_pad = jnp.pad(b_router.reshape(1, ne), lane_pad, constant_values=-jnp.inf)
    return pl.pallas_call(
        functools.partial(_ln1_kernel, alpha=alpha, tiles_a=tiles_a),
        grid=(tiles_a + tiles_b,),
        in_specs=[
            pl.BlockSpec((tm, d), row_a),
            pl.BlockSpec((tm, d), row_b),
            pl.BlockSpec((d, d), const),
            pl.BlockSpec((tm, d), row_a),
            pl.BlockSpec((tm, d), row_b),
            pl.BlockSpec((1, N_MOD, d), mod_row),
            pl.BlockSpec((1, d), const),
            pl.BlockSpec((1, d), const),
            pl.BlockSpec((d, 2 * LANES_V7X), const),
            pl.BlockSpec((1, LANES_V7X), const),
        ],
        out_specs=[
            pl.BlockSpec((tm, d), row),
            pl.BlockSpec((tm, d), row),
            pl.BlockSpec((tm, LANES_V7X), row),
            pl.BlockSpec((tm, LANES_V7X), row),
        ],
        out_shape=[
            jax.ShapeDtypeStruct((t, d), F32),
            jax.ShapeDtypeStruct((t, d), F32),
            jax.ShapeDtypeStruct((t, LANES_V7X), jnp.int32),
            jax.ShapeDtypeStruct((t, LANES_V7X), F32),
        ],
        compiler_params=_params("parallel"),
        name="oproj_ln1_router",
    )(mga, mgb, w_o, xa, xb, mod3, ln_g.reshape(1, d), ln_b.reshape(1, d), w_split, b_pad)


def _start_row_gather(src_hbm, idx_ref, dst_ref, sem, dst_slot):
    for r in range(dst_ref.shape[1]):
        pltpu.make_async_copy(src_hbm.at[pl.ds(idx_ref[0, 0, r], 1), :],
                              dst_ref.at[dst_slot, pl.ds(r, 1), :], sem.at[dst_slot]).start()


def _moe_kernel(be_ref, used_ref, tok_ref, tok_next_ref, g_ref, h_hbm, wg_ref, wu_ref, bg_ref, bu_ref, wd_ref, bd_ref,
                o_ref, xg_ref, acc_ref, sem, *, nf):
    del be_ref
    i = pl.program_id(0)
    f = pl.program_id(1)
    nblk = pl.num_programs(0)
    bm = xg_ref.shape[1]
    used = used_ref[i] > 0
    slot = i % 2

    @pl.when(jnp.logical_and(f == 0, jnp.logical_and(i == 0, used)))
    def _():
        _start_row_gather(h_hbm, tok_ref, xg_ref, sem, 0)

    @pl.when(jnp.logical_and(f == 0, used))
    def _():
        pltpu.make_async_copy(h_hbm.at[pl.ds(0, bm), :], xg_ref.at[slot], sem.at[slot]).wait()

    nxt = jnp.minimum(i + 1, nblk - 1)

    prefetch = jnp.logical_and(f == 0, jnp.logical_and(i + 1 < nblk, used_ref[nxt] > 0))
    for next_slot in (0, 1):
        @pl.when(jnp.logical_and(prefetch, slot == 1 - next_slot))
        def _():
            _start_row_gather(h_hbm, tok_next_ref, xg_ref, sem, next_slot)

    def down_contribution():
        x = xg_ref[slot].astype(MXU_DTYPE)
        gate = jnp.dot(x, wg_ref[0, 0], preferred_element_type=F32) + bg_ref[0]
        up = jnp.dot(x, wu_ref[0, 0], preferred_element_type=F32) + bu_ref[0]
        gate = jnp.minimum(gate, SWIGLU_LIMIT)
        up = jnp.clip(up, -SWIGLU_LIMIT, SWIGLU_LIMIT)
        act = (up + 1.0) * gate * _sigmoid(SWIGLU_ALPHA * gate)
        return jnp.dot(act.astype(MXU_DTYPE), wd_ref[0], preferred_element_type=F32)

    first = f == 0
    last = f == nf - 1

    if nf > 1:
        @pl.when(jnp.logical_and(used, first))
        def _():
            acc_ref[...] = down_contribution()

        @pl.when(jnp.logical_and(used, jnp.logical_and(jnp.logical_not(first), jnp.logical_not(last))))
        def _():
            acc_ref[...] += down_contribution()

    @pl.when(jnp.logical_and(used, last))
    def _():
        total = down_contribution()
        if nf > 1:
            total = total + acc_ref[...]
        o_ref[...] = (total + bd_ref[0]) * g_ref[...]

    @pl.when(jnp.logical_and(last, jnp.logical_not(used)))
    def _():
        o_ref[...] = jnp.zeros(o_ref.shape, F32)


def _moe(h2, blk_e, blk_used, slot_tok, slot_g, w_gu, b_gu, w_down, b_down, bm):
    t, d = h2.shape
    ne, two_nf, _, tf = w_gu.shape
    nf = two_nf // 2
    two_ff = two_nf * tf
    p = slot_tok.shape[0]
    nblk = p // bm

    def ftile(i, f, bu):
        return jnp.where(bu[i] > 0, f, nf - 1)

    grid_spec = pltpu.PrefetchScalarGridSpec(
        num_scalar_prefetch=2,
        grid=(nblk, nf),
        in_specs=[
            pl.BlockSpec((1, 1, bm), lambda i, f, be, bu: (i, 0, 0), memory_space=pltpu.SMEM),
            pl.BlockSpec((1, 1, bm), lambda i, f, be, bu: (jnp.minimum(i + 1, nblk - 1), 0, 0),
                         memory_space=pltpu.SMEM),
            pl.BlockSpec((bm, 1), lambda i, f, be, bu: (i, 0)),
            pl.BlockSpec(memory_space=pl.ANY),
            pl.BlockSpec((1, 1, d, tf), lambda i, f, be, bu: (be[i], ftile(i, f, bu), 0, 0)),
            pl.BlockSpec((1, 1, d, tf), lambda i, f, be, bu: (be[i], nf + ftile(i, f, bu), 0, 0)),
            pl.BlockSpec((1, 1, tf), lambda i, f, be, bu: (be[i], 0, ftile(i, f, bu))),
            pl.BlockSpec((1, 1, tf), lambda i, f, be, bu: (be[i], 0, nf + ftile(i, f, bu))),
            pl.BlockSpec((1, tf, d), lambda i, f, be, bu: (be[i], ftile(i, f, bu), 0)),
            pl.BlockSpec((1, 1, d), lambda i, f, be, bu: (be[i], 0, 0)),
        ],
        out_specs=pl.BlockSpec((bm, d), lambda i, f, be, bu: (i, 0)),
        scratch_shapes=[
            pltpu.VMEM((2, bm, d), F32),
            pltpu.VMEM((bm, d), F32),
            pltpu.SemaphoreType.DMA((2,)),
        ],
    )
    slot_tok3 = slot_tok.reshape(nblk, 1, bm)
    return pl.pallas_call(
        functools.partial(_moe_kernel, nf=nf),
        grid_spec=grid_spec,
        out_shape=jax.ShapeDtypeStruct((p, d), F32),
        compiler_params=_params("arbitrary", "arbitrary"),
        name="moe_experts",
    )(blk_e, blk_used, slot_tok3, slot_tok3, slot_g.reshape(p, 1), h2,
      w_gu, w_gu, b_gu.reshape(ne, 1, two_ff), b_gu.reshape(ne, 1, two_ff), w_down, b_down.reshape(ne, 1, d))


def _final_kernel(dst_ref, dst_next_ref, x1_ref, mod_ref, lng_ref, lnb_ref, yb_hbm, o_ref, buf_ref, sem, *, alpha):
    i = pl.program_id(0)
    nt = pl.num_programs(0)
    tm = x1_ref.shape[0]
    nrow = buf_ref.shape[1]
    slot = i % 2

    @pl.when(i == 0)
    def _():
        _start_row_gather(yb_hbm, dst_ref, buf_ref, sem, 0)

    for next_slot in (0, 1):
        @pl.when(jnp.logical_and(i + 1 < nt, slot == 1 - next_slot))
        def _():
            _start_row_gather(yb_hbm, dst_next_ref, buf_ref, sem, next_slot)

    pltpu.make_async_copy(yb_hbm.at[pl.ds(0, nrow), :], buf_ref.at[slot], sem.at[slot]).wait()
    y = buf_ref[slot, 0:tm, :]
    for k in range(1, TOP_K):
        y = y + buf_ref[slot, k * tm:(k + 1) * tm, :]
    g2 = mod_ref[0, 5:6, :]
    o_ref[...] = _layer_norm_rows(alpha * x1_ref[...] + g2 * y, lng_ref[...], lnb_ref[...])


def _combine_ln2(dst, x1, mod3, ln_g, ln_b, yb, seq, alpha, row0, batch0, rows):
    d = x1.shape[1]
    tm = _tile(seq, TM_FINAL)
    per = seq // tm
    nt = rows // tm
    tile0 = row0 // tm
    assert row0 % tm == 0
    nt_all = dst.shape[0] // tm
    dst_tiles = dst.reshape(nt_all, tm, TOP_K).transpose(0, 2, 1).reshape(nt_all, 1, TOP_K * tm)
    return pl.pallas_call(
        functools.partial(_final_kernel, alpha=alpha),
        grid=(nt,),
        in_specs=[
            pl.BlockSpec((1, 1, TOP_K * tm), lambda i: (tile0 + i, 0, 0), memory_space=pltpu.SMEM),
            pl.BlockSpec((1, 1, TOP_K * tm), lambda i: (tile0 + jnp.minimum(i + 1, nt - 1), 0, 0),
                         memory_space=pltpu.SMEM),
            pl.BlockSpec((tm, d), lambda i: (tile0 + i, 0)),
            pl.BlockSpec((1, N_MOD, d), lambda i: (batch0 + i // per, 0, 0)),
            pl.BlockSpec((1, d), lambda i: (0, 0)),
            pl.BlockSpec((1, d), lambda i: (0, 0)),
            pl.BlockSpec(memory_space=pl.ANY),
        ],
        out_specs=pl.BlockSpec((tm, d), lambda i: (i, 0)),
        out_shape=jax.ShapeDtypeStruct((rows, d), F32),
        scratch_shapes=[
            pltpu.VMEM((2, TOP_K * tm, d), F32),
            pltpu.SemaphoreType.DMA((2,)),
        ],
        compiler_params=_params("arbitrary"),
        name="combine_ln2",
    )(dst_tiles, dst_tiles, x1, mod3, ln_g.reshape(1, d), ln_b.reshape(1, d), yb)


def _route_slots(top_i, gates, n_experts, bm):
    t = top_i.shape[0]
    na = t * TOP_K
    flat_e = top_i.reshape(-1)
    order = jnp.argsort(flat_e).astype(jnp.int32)
    rank = jnp.argsort(order).astype(jnp.int32)
    experts = jnp.arange(n_experts, dtype=jnp.int32)
    onehot = experts[:, None] == flat_e[None, :]
    counts = jnp.sum(onehot.astype(jnp.int32), axis=1)
    starts = jnp.cumsum(counts) - counts
    padded = ((counts + bm - 1) // bm) * bm
    pends = jnp.cumsum(padded)
    pstarts = pends - padded
    shift = jnp.sum(jnp.where(onehot, (pstarts - starts)[:, None], 0), axis=0)
    dest = (shift + rank).reshape(t, TOP_K)
    p = na + n_experts * bm
    nblk = p // bm
    blk_start = jnp.arange(nblk, dtype=jnp.int32) * bm
    blk_e = jnp.minimum(jnp.sum((blk_start[:, None] >= pends[None, :]).astype(jnp.int32), axis=1), n_experts - 1)
    blk_within = blk_start - pstarts[blk_e]
    row = jnp.arange(bm, dtype=jnp.int32)[None, :]
    within = blk_within[:, None] + row
    real = (within < counts[blk_e][:, None]).reshape(p)
    assign = order[jnp.clip(starts[blk_e][:, None] + within, 0, na - 1).reshape(p)]
    slot_tok = jnp.where(real, assign // TOP_K, 0)
    slot_g = jnp.where(real, gates.reshape(-1)[assign], 0.0)
    blk_used = (blk_start < pends[-1]).astype(jnp.int32)
    return slot_tok, slot_g, dest, blk_e, blk_used


def _pad_rope_cols(w):
    half = ROPE_B // 2
    z = jnp.zeros(w.shape[:-1] + (LANES_V7X // 2 - half,), w.dtype)
    return jnp.concatenate([w[..., :half], z, w[..., half:], z], axis=-1)


def _rope_tables(seq):
    half = ROPE_B // 2
    pos = jnp.arange(seq, dtype=F32)
    freqs = ROPE_THETA ** (-jnp.arange(half, dtype=F32) / half)
    ang = pos[:, None] * freqs[None, :]
    cos = jnp.cos(ang)
    sin = jnp.sin(ang)
    z = jnp.zeros((seq, LANES_V7X // 2 - half), F32)
    return (jnp.concatenate([cos, z, cos, z], axis=1), jnp.concatenate([-sin, z, sin, z], axis=1))


def _cast_tiles_kernel(w_ref, o_ref):
    o_ref[0, 0] = w_ref[0].astype(o_ref.dtype)


def _cast_gate_up_tiles(w_gu, tf):
    ne, d, two_ff = w_gu.shape
    nt = two_ff // tf
    return pl.pallas_call(
        _cast_tiles_kernel,
        grid=(ne, nt),
        in_specs=[pl.BlockSpec((1, d, tf), lambda e, j: (e, 0, j))],
        out_specs=pl.BlockSpec((1, 1, d, tf), lambda e, j: (e, j, 0, 0)),
        out_shape=jax.ShapeDtypeStruct((ne, nt, d, tf), MXU_DTYPE),
        compiler_params=_params("parallel", "parallel"),
        name="cast_gate_up_tiles",
    )(w_gu)


def _prepare_layer(w_in, w_uq, w_ukv, w_a_out, w_b_out, w_o, w_gu, w_down):
    d = w_in.shape[0]
    qa_w = N_Q_A * HD_A
    ka_w = N_KV_A * HD_A
    q_lora = w_uq.shape[0]
    kv_lora = w_ukv.shape[0]
    assert q_lora == kv_lora
    group = N_Q_A // N_KV_A
    offs = [0]
    for wdt in (qa_w, ka_w, ka_w, q_lora, kv_lora, ROPE_B, d, d):
        offs.append(offs[-1] + wdt)
    assert offs[-1] == w_in.shape[1]
    part = lambda i: w_in[:, offs[i]:offs[i + 1]]
    w_qa = part(0).reshape(d, N_KV_A, group, HD_A).transpose(0, 2, 1, 3).reshape(d, qa_w) * (HD_A ** -0.5 * LOG2_E)
    cast = lambda w: w.astype(MXU_DTYPE)
    w_uq3 = w_uq.reshape(q_lora, N_H_B, NOPE_B + ROPE_B)
    w_uq_p = jnp.concatenate([w_uq3[..., :NOPE_B], _pad_rope_cols(w_uq3[..., NOPE_B:])], axis=-1)
    w_ukv3 = w_ukv.reshape(kv_lora, N_H_B, NOPE_B + V_B)
    tf_moe = _tile(w_gu.shape[2] // 2, TF_MOE)
    return dict(
        w_qkv=cast(jnp.concatenate([w_qa, part(1), part(2)], axis=1)),
        w_lat=cast(jnp.concatenate([part(3), part(4)], axis=1)),
        w_kr=cast(_pad_rope_cols(part(5))),
        w_gates=cast(jnp.concatenate([part(6), part(7)], axis=1)),
        w_uq=cast(w_uq_p.reshape(q_lora, N_H_B * (NOPE_B + LANES_V7X))),
        w_uk=cast(w_ukv3[..., :NOPE_B].reshape(kv_lora, N_H_B * NOPE_B)),
        w_uv_t=cast(w_ukv3[..., NOPE_B:].reshape(kv_lora, N_H_B * V_B).T),
        w_a_out=cast(w_a_out.reshape(N_KV_A, group, HD_A, d).transpose(1, 0, 2, 3).reshape(qa_w, d)),
        w_b_out=cast(w_b_out),
        w_o=cast(w_o),
        w_gu=_cast_gate_up_tiles(w_gu, tf_moe),
        w_down=cast(w_down),
    )


def _mixers(x2d, mod3, wts, sinks_a, q_norm_g, kv_norm_g, batch, seq):
    tables = _rope_tables(seq)
    h = _modulate(x2d, mod3, seq)
    qkv = _matmul(h, wts['w_qkv'], MXU_DTYPE, "proj_qkv_a")
    lat = _matmul(h, wts['w_lat'], F32, "proj_latents")
    gates = _matmul(h, wts['w_gates'], MXU_DTYPE, "proj_gates")
    kr = _matmul_rope(h, wts['w_kr'], tables[0], tables[1], seq, "proj_k_rope")

    oa = _window_attention(qkv, sinks_a, batch, seq)

    qscale = (NOPE_B + ROPE_B) ** -0.5 * LOG2_E
    q = _q_up(lat, q_norm_g, wts['w_uq'], tables, seq, qscale)
    kcat = _k_up(lat, kv_norm_g, wts['w_uk'], kr, seq)
    vt = _v_up_t(lat, kv_norm_g, wts['w_uv_t'], _tile(seq, TKC_MLA))
    ob = _mla_attention(q, kcat, vt, batch, seq)
    return _merge(oa, ob, wts['w_a_out'], wts['w_b_out'], gates)


def _encoder_layer(xs, mod, wts, sinks_a, q_norm_g, kv_norm_g, ln1_g, ln1_b, w_router, b_router,
                   b_gu, b_down, ln2_g, ln2_b, alpha):
    d = xs[0].shape[-1]
    batches = [x.shape[0] for x in xs]
    seqs = [x.shape[1] for x in xs]
    rows = [b * s for b, s in zip(batches, seqs)]
    x2d = [x.reshape(r, d) for x, r in zip(xs, rows)]
    mod3 = mod.reshape(mod.shape[0], N_MOD, d)
    merged = []
    b0 = 0
    for x, b, s in zip(x2d, batches, seqs):
        merged.append(_mixers(x, mod3[b0:b0 + b], wts, sinks_a, q_norm_g, kv_norm_g, b, s))
        b0 += b

    x1, h2, ridx, rgate = _oproj_ln1_router(merged, wts['w_o'], x2d, mod3, ln1_g, ln1_b, w_router, b_router,
                                            seqs, alpha)
    n_experts = w_router.shape[1]
    bm = _tile(sum(rows), BM_MOE)
    slot_tok, slot_g, dest, blk_e, blk_used = _route_slots(ridx[:, :TOP_K], rgate[:, :TOP_K], n_experts, bm)
    yb = _moe(h2, blk_e, blk_used, slot_tok, slot_g, wts['w_gu'], b_gu, wts['w_down'], b_down, bm)

    outs = []
    row0 = b0 = 0
    for b, s, r in zip(batches, seqs, rows):
        out = _combine_ln2(dest, x1, mod3, ln2_g, ln2_b, yb, s, alpha, row0, b0, r)
        outs.append(out.reshape(b, s, d))
        row0 += r
        b0 += b
    return outs


def kernel(x_prompt, x_sample, c_prompt, c_sample, w_ada, b_ada, w_in, sinks_a, q_norm_g, kv_norm_g, w_uq, w_ukv, w_a_out, w_b_out, w_o, ln1_g, ln1_b, w_router, b_router, w_gu, b_gu, w_down, b_down, ln2_g, ln2_b):
    depth = w_ada.shape[0]
    alpha = (2.0 * depth) ** 0.25
    pad = (-(c_prompt.shape[0] + c_sample.shape[0])) % SUBLANES_V7X
    c_all = jnp.concatenate([c_prompt, c_sample, jnp.zeros((pad, c_prompt.shape[1]), F32)], axis=0)
    ys = [x_prompt, x_sample]
    for l in range(depth):
        mod = _ada(c_all, w_ada[l], b_ada[l])
        wts = _prepare_layer(w_in[l], w_uq[l], w_ukv[l], w_a_out[l], w_b_out[l], w_o[l], w_gu[l], w_down[l])
        ys = _encoder_layer(ys, mod, wts, sinks_a[l], q_norm_g[l], kv_norm_g[l], ln1_g[l], ln1_b[l],
                            w_router[l], b_router[l], b_gu[l], b_down[l], ln2_g[l], ln2_b[l], alpha)
    return (ys[0], ys[1])
```
